```python
import math, functools
import jax, jax.numpy as jnp
from jax import lax
import numpy as np

D_MODEL = 1024
BATCH = 8
SEQ = 2048
DEPTH = 1
DEC_BATCH = 128
DEC_SEQ = 1
PAST_LEN = 8192
PAGE_SIZE = 128

N_HEADS = 16
HEAD_DIM = 64
D_ATTN = N_HEADS * HEAD_DIM
BLOCK_SIZE = 256
TOP_K = 3
N_SLOTS = TOP_K + 1
Q_CHUNK = 16
N_BUCKETS = 32
MAX_DISTANCE = 128
D_CONV = D_MODEL
CONV_WIDTH = 3
D_FF = 2816
RMS_EPS = 1e-6
SCALE = HEAD_DIM ** -0.5
SPLIT_POINTS = (D_ATTN, 2 * D_ATTN, 3 * D_ATTN, 3 * D_ATTN + D_CONV, 3 * D_ATTN + 2 * D_CONV, 3 * D_ATTN + 3 * D_CONV, 3 * D_ATTN + 3 * D_CONV + D_MODEL)
D_IN = 3 * D_ATTN + 3 * D_CONV + 2 * D_MODEL
POOL_NUM = 5
POOL_DEN = 4

kernel_name = 'moba_shortconv_macaron_hybrid_step'


def rmsnorm(x, g):
    xf = x.astype(jnp.float32)
    xf = xf * lax.rsqrt(jnp.mean(xf * xf, axis=-1, keepdims=True) + RMS_EPS)
    return (xf * g.astype(jnp.float32)).astype(x.dtype)


def swiglu(x, w_gate, w_up, w_down):
    return (jax.nn.silu(x @ w_gate) * (x @ w_up)) @ w_down


def t5_bucket(dist):
    n = jnp.maximum(dist, 0)
    max_exact = N_BUCKETS // 2
    nf = jnp.maximum(n, 1).astype(jnp.float32)
    large = max_exact + (jnp.log(nf / max_exact) / math.log(MAX_DISTANCE / max_exact) * (N_BUCKETS - max_exact)).astype(jnp.int32)
    return jnp.where(n < max_exact, n, jnp.minimum(large, N_BUCKETS - 1))


def select_blocks(q, kmean, qpos):
    nb = kmean.shape[2]
    scores = jnp.einsum('bthd,bhjd->bthj', q.astype(jnp.float32), kmean)
    n_past = qpos // BLOCK_SIZE
    is_past = jnp.arange(nb)[None, :] < n_past[:, None]
    scores = jnp.where(is_past[None, :, None, :], scores, -jnp.inf)
    if nb < TOP_K:
        scores = jnp.pad(scores, ((0, 0), (0, 0), (0, 0), (0, TOP_K - nb)), constant_values=-jnp.inf)
    _, idx = lax.top_k(scores, TOP_K)
    idx = jnp.minimum(idx, nb - 1).astype(jnp.int32)
    own = jnp.broadcast_to((qpos // BLOCK_SIZE).astype(jnp.int32)[None, :, None, None], idx.shape[:3] + (1,))
    blocks = jnp.concatenate([idx, own], axis=-1)
    slot = jnp.arange(N_SLOTS)[None, :]
    ok = (slot < jnp.minimum(n_past, TOP_K)[:, None]) | (slot == TOP_K)
    ok = jnp.broadcast_to(ok[None, :, None, :], blocks.shape)
    return blocks, ok


def moba_prompt(q, k, v, rel_bias):
    b, s, h, d = q.shape
    nb = -(-s // BLOCK_SIZE)
    pad = nb * BLOCK_SIZE - s
    kblk = jnp.pad(k, ((0, 0), (0, pad), (0, 0), (0, 0))).reshape(b, nb, BLOCK_SIZE, h, d).transpose(0, 3, 1, 2, 4)
    vblk = jnp.pad(v, ((0, 0), (0, pad), (0, 0), (0, 0))).reshape(b, nb, BLOCK_SIZE, h, d).transpose(0, 3, 1, 2, 4)
    kmean = jnp.mean(kblk, axis=3, dtype=jnp.float32)
    qpos = jnp.arange(s, dtype=jnp.int32)
    blocks, ok = select_blocks(q, kmean, qpos)
    n_chunks = s // Q_CHUNK
    bi = jnp.arange(b)[:, None, None, None]
    hi = jnp.arange(h)[None, None, :, None]
    bias_table = rel_bias.astype(jnp.float32)

    def to_chunks(a):
        return jnp.moveaxis(a.reshape((b, n_chunks, Q_CHUNK) + a.shape[2:]), 1, 0)

    def attend(args):
        qc, bc, okc, pc = args
        kg = kblk[bi, hi, bc]
        vg = vblk[bi, hi, bc]
        kpos = bc[..., None] * BLOCK_SIZE + jnp.arange(BLOCK_SIZE)
        dist = pc[None, :, None, None, None] - kpos
        logits = jnp.einsum('bqhd,bqhskd->bqhsk', qc, kg, preferred_element_type=jnp.float32) * SCALE
        logits = logits + bias_table[t5_bucket(dist), hi[..., None]]
        logits = jnp.where(okc[..., None] & (dist >= 0), logits, -jnp.inf)
        probs = jax.nn.softmax(logits, axis=(-2, -1))
        return jnp.einsum('bqhsk,bqhskd->bqhd', probs.astype(vg.dtype), vg)

    out = lax.map(attend, (to_chunks(q), to_chunks(blocks), to_chunks(ok), qpos.reshape(n_chunks, Q_CHUNK)))
    return jnp.moveaxis(out, 0, 1).reshape(b, s, h, d)


def moba_sample(q, k_new, v_new, cache_k, cache_v, page_table, page_key_sums, layer, rel_bias):
    db, t, h, d = q.shape
    n_pages = page_table.shape[1]
    page_size = cache_k.shape[2]
    past_len = n_pages * page_size
    ppb = BLOCK_SIZE // page_size
    nb = -(-(past_len + t) // BLOCK_SIZE)
    qpos = past_len + jnp.arange(t, dtype=jnp.int32)
    bias_table = rel_bias.astype(jnp.float32)
    page_sums = page_key_sums[layer, page_table]
    page_sums = jnp.pad(page_sums, ((0, 0), (0, nb * ppb - n_pages), (0, 0), (0, 0)))
    block_sums = page_sums.reshape(db, nb, ppb, h, d).sum(axis=2)
    onehot = ((qpos // BLOCK_SIZE)[:, None] == jnp.arange(nb)[None, :]).astype(jnp.float32)
    block_sums = block_sums + jnp.einsum('bthd,tj->bjhd', k_new.astype(jnp.float32), onehot)
    kmean = (block_sums / BLOCK_SIZE).transpose(0, 2, 1, 3)
    blocks, ok = select_blocks(q, kmean, qpos)
    bi = jnp.arange(db)[:, None, None, None, None]
    hi = jnp.arange(h)[None, None, :, None, None]
    lpage = jnp.minimum(blocks[..., None] * ppb + jnp.arange(ppb), n_pages - 1)
    phys = page_table[bi, lpage]
    rows = jnp.arange(page_size)
    kg = cache_k[layer, phys[..., None], rows, hi[..., None]].reshape(db, t, h, N_SLOTS, BLOCK_SIZE, d)
    vg = cache_v[layer, phys[..., None], rows, hi[..., None]].reshape(db, t, h, N_SLOTS, BLOCK_SIZE, d)
    kpos = blocks[..., None] * BLOCK_SIZE + jnp.arange(BLOCK_SIZE)
    dist = qpos[None, :, None, None, None] - kpos
    logit_past = jnp.einsum('bqhd,bqhskd->bqhsk', q, kg, preferred_element_type=jnp.float32) * SCALE
    logit_past = logit_past + bias_table[t5_bucket(dist), hi]
    mask_past = ok[..., None] & (dist >= 0) & (kpos < past_len)
    logit_past = jnp.where(mask_past, logit_past, -jnp.inf).reshape(db, t, h, N_SLOTS * BLOCK_SIZE)
    dist_new = qpos[:, None] - qpos[None, :]
    bias_new = bias_table[t5_bucket(dist_new)].transpose(0, 2, 1)
    logit_new = jnp.einsum('bqhd,bkhd->bqhk', q, k_new, preferred_element_type=jnp.float32) * SCALE + bias_new[None]
    kblock_new = qpos // BLOCK_SIZE
    in_sel = jnp.any(ok[..., None] & (blocks[..., None] == kblock_new), axis=3)
    mask_new = in_sel & (dist_new >= 0)[None, :, None, :]
    logit_new = jnp.where(mask_new, logit_new, -jnp.inf)
    probs = jax.nn.softmax(jnp.concatenate([logit_past, logit_new], axis=-1), axis=-1)
    p_past = probs[..., :N_SLOTS * BLOCK_SIZE].reshape(db, t, h, N_SLOTS, BLOCK_SIZE)
    p_new = probs[..., N_SLOTS * BLOCK_SIZE:]
    out = jnp.einsum('bqhsk,bqhskd->bqhd', p_past.astype(vg.dtype), vg)
    return out + jnp.einsum('bqhk,bkhd->bqhd', p_new.astype(v_new.dtype), v_new)


def short_conv(u, buf, w_conv):
    t = u.shape[1]
    up = jnp.concatenate([buf.astype(u.dtype), u], axis=1)
    out = w_conv[0] * up[:, 0:t]
    for i in range(1, CONV_WIDTH):
        out = out + w_conv[i] * up[:, i:i + t]
    return out, up[:, t:]


def layer_forward(x, conv_buf, attend, lw):
    (norm_ffn1, w_ffn1_gate, w_ffn1_up, w_ffn1_down, norm_mix, w_in, w_conv, w_branch_attn,
     w_branch_conv, w_out, norm_ffn2, w_ffn2_gate, w_ffn2_up, w_ffn2_down) = lw
    bsz, t = x.shape[0], x.shape[1]
    x = x + 0.5 * swiglu(rmsnorm(x, norm_ffn1), w_ffn1_gate, w_ffn1_up, w_ffn1_down)
    h = rmsnorm(x, norm_mix)
    q, k, v, gate_b, gate_c, u_in, g_attn, g_conv = jnp.split(h @ w_in, list(SPLIT_POINTS), axis=-1)
    q = q.reshape(bsz, t, N_HEADS, HEAD_DIM)
    k = k.reshape(bsz, t, N_HEADS, HEAD_DIM)
    v = v.reshape(bsz, t, N_HEADS, HEAD_DIM)
    o_attn = attend(q, k, v)
    conv_out, new_buf = short_conv(gate_c * u_in, conv_buf, w_conv)
    y_attn = o_attn.reshape(bsz, t, D_ATTN) @ w_branch_attn
    y_conv = (gate_b * conv_out) @ w_branch_conv
    merged = jax.nn.sigmoid(g_attn) * y_attn + jax.nn.sigmoid(g_conv) * y_conv
    x = x + merged @ w_out
    x = x + 0.5 * swiglu(rmsnorm(x, norm_ffn2), w_ffn2_gate, w_ffn2_up, w_ffn2_down)
    return x, k, v, new_buf


def setup_inputs(seed: int = 0) -> dict:
    key = jax.random.key(seed)
    ks = jax.random.split(key, 24)
    n_pages = PAST_LEN // PAGE_SIZE
    n_phys = (DEC_BATCH * n_pages * POOL_NUM) // POOL_DEN

    def dense(k, shape, fan_in):
        return jax.random.normal(k, shape, jnp.float32) * fan_in ** -0.5

    def gain(k, shape):
        return 1.0 + 0.05 * jax.random.normal(k, shape, jnp.float32)

    page_table = jax.random.permutation(ks[5], n_phys)[:DEC_BATCH * n_pages].reshape(DEC_BATCH, n_pages).astype(jnp.int32)
    return {
        'x_prompt': jax.random.normal(ks[0], (BATCH, SEQ, D_MODEL), jnp.float32),
        'x_sample': jax.random.normal(ks[1], (DEC_BATCH, DEC_SEQ, D_MODEL), jnp.float32),
        'cache_k': jax.random.normal(ks[2], (DEPTH, n_phys, PAGE_SIZE, N_HEADS, HEAD_DIM), jnp.float32),
        'cache_v': jax.random.normal(ks[3], (DEPTH, n_phys, PAGE_SIZE, N_HEADS, HEAD_DIM), jnp.float32),
        'state_conv': jax.random.normal(ks[4], (DEPTH, DEC_BATCH, CONV_WIDTH - 1, D_CONV), jnp.float32),
        'page_table': page_table,
        'rel_bias': 0.5 * jax.random.normal(ks[6], (N_BUCKETS, N_HEADS), jnp.float32),
        'norm_ffn1': gain(ks[7], (DEPTH, D_MODEL)),
        'w_ffn1_gate': dense(ks[8], (DEPTH, D_MODEL, D_FF), D_MODEL),
        'w_ffn1_up': dense(ks[9], (DEPTH, D_MODEL, D_FF), D_MODEL),
        'w_ffn1_down': dense(ks[10], (DEPTH, D_FF, D_MODEL), D_FF),
        'norm_mix': gain(ks[11], (DEPTH, D_MODEL)),
        'w_in': dense(ks[12], (DEPTH, D_MODEL, D_IN), D_MODEL),
        'w_conv': dense(ks[13], (DEPTH, CONV_WIDTH, D_CONV), CONV_WIDTH),
        'w_branch_attn': dense(ks[14], (DEPTH, D_ATTN, D_MODEL), D_ATTN),
        'w_branch_conv': dense(ks[15], (DEPTH, D_CONV, D_MODEL), D_CONV),
        'w_out': dense(ks[16], (DEPTH, D_MODEL, D_MODEL), D_MODEL),
        'norm_ffn2': gain(ks[17], (DEPTH, D_MODEL)),
        'w_ffn2_gate': dense(ks[18], (DEPTH, D_MODEL, D_FF), D_MODEL),
        'w_ffn2_up': dense(ks[19], (DEPTH, D_MODEL, D_FF), D_MODEL),
        'w_ffn2_down': dense(ks[20], (DEPTH, D_FF, D_MODEL), D_FF),
        'norm_final': gain(ks[21], (D_MODEL,)),
    }


def reference(x_prompt, x_sample, cache_k, cache_v, state_conv, page_table, rel_bias,
              norm_ffn1, w_ffn1_gate, w_ffn1_up, w_ffn1_down, norm_mix, w_in, w_conv,
              w_branch_attn, w_branch_conv, w_out, norm_ffn2, w_ffn2_gate, w_ffn2_up,
              w_ffn2_down, norm_final):
    page_key_sums = jnp.sum(cache_k, axis=2, dtype=jnp.float32)
    prompt_buf = jnp.zeros((x_prompt.shape[0], CONV_WIDTH - 1, D_CONV), x_prompt.dtype)
    attend_prompt = functools.partial(moba_prompt, rel_bias=rel_bias)
    xp, xs = x_prompt, x_sample
    kp, vp, cp, ksm, vsm, csm = [], [], [], [], [], []
    for layer in range(DEPTH):
        lw = (norm_ffn1[layer], w_ffn1_gate[layer], w_ffn1_up[layer], w_ffn1_down[layer],
              norm_mix[layer], w_in[layer], w_conv[layer], w_branch_attn[layer],
              w_branch_conv[layer], w_out[layer], norm_ffn2[layer], w_ffn2_gate[layer],
              w_ffn2_up[layer], w_ffn2_down[layer])
        attend_sample = functools.partial(moba_sample, cache_k=cache_k, cache_v=cache_v,
                                          page_table=page_table, page_key_sums=page_key_sums,
                                          layer=layer, rel_bias=rel_bias)
        xp, k_l, v_l, c_l = layer_forward(xp, prompt_buf, attend_prompt, lw)
        kp.append(k_l)
        vp.append(v_l)
        cp.append(c_l)
        xs, k_l, v_l, c_l = layer_forward(xs, state_conv[layer], attend_sample, lw)
        ksm.append(k_l)
        vsm.append(v_l)
        csm.append(c_l)
    y_prompt = rmsnorm(xp, norm_final)
    y_sample = rmsnorm(xs, norm_final)
    return (y_prompt, y_sample, jnp.stack(kp), jnp.stack(vp), jnp.stack(cp), jnp.stack(ksm), jnp.stack(vsm), jnp.stack(csm))
```

```python
import functools
import math

import jax
import jax.numpy as jnp
from jax import lax
from jax.experimental import pallas as pl
from jax.experimental.pallas import tpu as pltpu

N_HEADS = 16
HEAD_DIM = 64
BLOCK_SIZE = 256
TOP_K = 3
N_BUCKETS = 32
MAX_DISTANCE = 128
CONV_WIDTH = 3
RMS_EPS = 1e-6
SCALE = HEAD_DIM ** -0.5

V7X_LANES = 128
V7X_SUBLANES = 8
V7X_VMEM_LIMIT_BYTES = 56 * 1024 * 1024

HEADS_PER_GROUP = V7X_LANES // HEAD_DIM
FF_CHUNK = 256
PROMPT_ROWS = 512
SCAN_PAGES = 16

F32 = jnp.float32
BF16 = jnp.bfloat16
NEG_INF = float("-inf")


def _const_spec(shape):
    nd = len(shape)
    return pl.BlockSpec(shape, lambda *_: (0,) * nd, pipeline_mode=pl.Buffered(1))


def _rms(x, g):
    ms = jnp.mean(x * x, axis=-1, keepdims=True)
    return (x * lax.rsqrt(ms + RMS_EPS)) * g


def _dot(a, b):
    return jnp.dot(a, b, preferred_element_type=F32)


def _dot_nt(a, b):
    return lax.dot_general(a, b, (((1,), (1,)), ((), ())), preferred_element_type=F32)


def _swiglu(h, wg_ref, wu_ref, wd_ref):
    d_ff = wg_ref.shape[1]
    acc = None
    for c in range(d_ff // FF_CHUNK):
        sl = slice(c * FF_CHUNK, (c + 1) * FF_CHUNK)
        gate = _dot(h, wg_ref[:, sl])
        up = _dot(h, wu_ref[:, sl])
        act = (gate * jax.nn.sigmoid(gate) * up).astype(BF16)
        part = _dot(act, wd_ref[sl, :])
        acc = part if acc is None else acc + part
    return acc


def _ffn_kernel(x_ref, g_ref, wg_ref, wu_ref, wd_ref, o_ref):
    x = x_ref[...]
    h = _rms(x, g_ref[...]).astype(BF16)
    o_ref[...] = x + 0.5 * _swiglu(h, wg_ref, wu_ref, wd_ref)


def _ffn(x, g, wg, wu, wd, rows):
    n, d = x.shape
    d_ff = wg.shape[1]
    row_spec = pl.BlockSpec((rows, d), lambda i: (i, 0))
    return pl.pallas_call(
        _ffn_kernel,
        grid=(n // rows,),
        in_specs=[row_spec, _const_spec((1, d)), _const_spec((d, d_ff)), _const_spec((d, d_ff)),
                  _const_spec((d_ff, d))],
        out_specs=row_spec,
        out_shape=jax.ShapeDtypeStruct((n, d), F32),
        compiler_params=pltpu.CompilerParams(dimension_semantics=("arbitrary",),
                                             vmem_limit_bytes=V7X_VMEM_LIMIT_BYTES),
        name="ffn",
    )(x, g, wg, wu, wd)


def _proj(h, win_ref, n):
    d = h.shape[1]
    return _dot(h, win_ref[:, n * d:(n + 1) * d])


def _mix_in_tail(h, win_ref, wbc_ref, gate_b, conv, ga_ref, zc_ref):
    y_conv = _dot((gate_b * conv).astype(BF16), wbc_ref[...])
    ga_ref[...] = jax.nn.sigmoid(_proj(h, win_ref, 6))
    zc_ref[...] = jax.nn.sigmoid(_proj(h, win_ref, 7)) * y_conv


def _mix_in_prompt_kernel(x_ref, g_ref, win_ref, wconv_ref, wbc_ref,
                          q_ref, k_ref, v_ref, ga_ref, zc_ref, cs_ref, cbuf, *, tiles_per_seq):
    i = pl.program_id(0)
    rows, d = x_ref.shape
    h = _rms(x_ref[...], g_ref[...]).astype(BF16)
    q_ref[...] = _proj(h, win_ref, 0)
    k_ref[...] = _proj(h, win_ref, 1)
    v_ref[...] = _proj(h, win_ref, 2)
    gate_b = _proj(h, win_ref, 3)
    cu = _proj(h, win_ref, 4) * _proj(h, win_ref, 5)

    @pl.when(i % tiles_per_seq == 0)
    def _():
        cbuf[0:V7X_SUBLANES, :] = jnp.zeros((V7X_SUBLANES, d), F32)

    cbuf[V7X_SUBLANES:V7X_SUBLANES + rows, :] = cu
    w = wconv_ref[...]
    conv = (w[0:1] * cbuf[V7X_SUBLANES - 2:V7X_SUBLANES - 2 + rows, :]
            + w[1:2] * cbuf[V7X_SUBLANES - 1:V7X_SUBLANES - 1 + rows, :]
            + w[2:3] * cu)
    last = cbuf[rows + V7X_SUBLANES - 2:rows + V7X_SUBLANES, :]
    cs_ref[0] = last
    cbuf[V7X_SUBLANES - 2:V7X_SUBLANES, :] = last
    _mix_in_tail(h, win_ref, wbc_ref, gate_b, conv, ga_ref, zc_ref)


def _mix_in_sample_kernel(x_ref, g_ref, win_ref, wconv_ref, wbc_ref, p2_ref, p1_ref,
                          q_ref, k_ref, v_ref, ga_ref, zc_ref, cu_ref):
    h = _rms(x_ref[...], g_ref[...]).astype(BF16)
    q_ref[...] = _proj(h, win_ref, 0)
    k_ref[...] = _proj(h, win_ref, 1)
    v_ref[...] = _proj(h, win_ref, 2)
    gate_b = _proj(h, win_ref, 3)
    cu = _proj(h, win_ref, 4) * _proj(h, win_ref, 5)
    cu_ref[...] = cu
    w = wconv_ref[...]
    conv = w[0:1] * p2_ref[...] + w[1:2] * p1_ref[...] + w[2:3] * cu
    _mix_in_tail(h, win_ref, wbc_ref, gate_b, conv, ga_ref, zc_ref)


def _mix_in_prompt(x, g, win, wconv, wbc, batch, seq, rows):
    n, d = x.shape
    tiles_per_seq = seq // rows
    row_spec = pl.BlockSpec((rows, d), lambda i: (i, 0))
    row_shape = jax.ShapeDtypeStruct((n, d), F32)
    return pl.pallas_call(
        functools.partial(_mix_in_prompt_kernel, tiles_per_seq=tiles_per_seq),
        grid=(n // rows,),
        in_specs=[row_spec, _const_spec((1, d)), _const_spec(win.shape), _const_spec(wconv.shape),
                  _const_spec(wbc.shape)],
        out_specs=[row_spec] * 5 + [pl.BlockSpec((1, CONV_WIDTH - 1, d), lambda i: (i // tiles_per_seq, 0, 0))],
        out_shape=[row_shape] * 5 + [jax.ShapeDtypeStruct((batch, CONV_WIDTH - 1, d), F32)],
        scratch_shapes=[pltpu.VMEM((rows + V7X_SUBLANES, d), F32)],
        compiler_params=pltpu.CompilerParams(dimension_semantics=("arbitrary",),
                                             vmem_limit_bytes=V7X_VMEM_LIMIT_BYTES),
        name="mix_in_prompt",
    )(x, g, win, wconv, wbc)


def _mix_in_sample(x, g, win, wconv, wbc, prev2, prev1):
    n, d = x.shape
    full = pl.BlockSpec((n, d), lambda i: (0, 0))
    row_shape = jax.ShapeDtypeStruct((n, d), F32)
    return pl.pallas_call(
        _mix_in_sample_kernel,
        grid=(1,),
        in_specs=[full, _const_spec((1, d)), _const_spec(win.shape), _const_spec(wconv.shape),
                  _const_spec(wbc.shape), full, full],
        out_specs=[full] * 6,
        out_shape=[row_shape] * 6,
        compiler_params=pltpu.CompilerParams(dimension_semantics=("arbitrary",),
                                             vmem_limit_bytes=V7X_VMEM_LIMIT_BYTES),
        name="mix_in_sample",
    )(x, g, win, wconv, wbc, prev2, prev1)


def _t5_bucket(dist):
    n = jnp.maximum(dist, 0)
    max_exact = N_BUCKETS // 2
    nf = jnp.maximum(n, 1).astype(F32)
    large = max_exact + (jnp.log(nf / max_exact) / math.log(MAX_DISTANCE / max_exact)
                         * (N_BUCKETS - max_exact)).astype(jnp.int32)
    return jnp.where(n < max_exact, n, jnp.minimum(large, N_BUCKETS - 1))


def _bias_lookup(dist, table_row):
    bucket = _t5_bucket(dist)
    bias = jnp.zeros(dist.shape, F32)
    for b in range(N_BUCKETS):
        bias = jnp.where(bucket == b, table_row(b), bias)
    return bias


def _bias_tiles_kernel(rb_ref, o_ref):
    h = pl.program_id(0)
    kk = lax.broadcasted_iota(jnp.int32, (BLOCK_SIZE, BLOCK_SIZE), 0)
    qq = lax.broadcasted_iota(jnp.int32, (BLOCK_SIZE, BLOCK_SIZE), 1)
    for t in range(2):
        o_ref[0, t] = _bias_lookup(qq - kk + t * BLOCK_SIZE, lambda b: rb_ref[b, h])


def _bias_tiles(rel_bias):
    return pl.pallas_call(
        _bias_tiles_kernel,
        grid=(N_HEADS,),
        in_specs=[pl.BlockSpec(memory_space=pltpu.SMEM)],
        out_specs=pl.BlockSpec((1, 2, BLOCK_SIZE, BLOCK_SIZE), lambda h: (h, 0, 0, 0)),
        out_shape=jax.ShapeDtypeStruct((N_HEADS, 2, BLOCK_SIZE, BLOCK_SIZE), F32),
        name="bias_tiles",
    )(rel_bias)


def _bias_last_block_kernel(rb_ref, o_ref):
    r = lax.broadcasted_iota(jnp.int32, o_ref.shape, 0)
    o_ref[...] = _bias_lookup(BLOCK_SIZE - r, lambda b: rb_ref[b:b + 1, :])


def _bias_last_block(rel_bias):
    return pl.pallas_call(
        _bias_last_block_kernel,
        out_shape=jax.ShapeDtypeStruct((BLOCK_SIZE, N_HEADS), F32),
        name="bias_last_block",
    )(rel_bias)


def _attn_prompt_kernel(rb_ref, q_ref, k_ref, v_ref, bt_ref, o_ref, kmean_sc, vt_sc, sel_sc):
    hg = pl.program_id(1)
    qb = pl.program_id(2)
    n_blk = k_ref.shape[1] // BLOCK_SIZE

    @pl.when(qb == 0)
    def _():
        means = []
        for j in range(n_blk):
            rows = slice(j * BLOCK_SIZE, (j + 1) * BLOCK_SIZE)
            means.append(jnp.mean(k_ref[0, rows, :], axis=0, keepdims=True))
            vt_sc[j] = v_ref[0, rows, :].T.astype(BF16)
        kmean_sc[...] = jnp.concatenate(means, axis=0)

    q = q_ref[0] * SCALE
    lane = lax.broadcasted_iota(jnp.int32, q.shape, 1)
    blk_row = lax.broadcasted_iota(jnp.int32, (n_blk, BLOCK_SIZE), 0)
    kmean = kmean_sc[...].astype(BF16)
    qms = []
    for i in range(HEADS_PER_GROUP):
        qm = jnp.where(lane // HEAD_DIM == i, q, 0.0).astype(BF16)
        qms.append(qm)
        st = _dot_nt(kmean, qm)
        rank = jnp.zeros(st.shape, jnp.int32)
        for j in range(n_blk):
            sj = st[j:j + 1, :]
            beats = (sj > st) | ((sj == st) & (j < blk_row))
            rank = rank + jnp.where(beats & (j < qb), 1, 0)
        sel_sc[i] = jnp.where((rank < TOP_K) & (blk_row < qb), 1.0, 0.0)

    kk = lax.broadcasted_iota(jnp.int32, (BLOCK_SIZE, BLOCK_SIZE), 0)
    qq = lax.broadcasted_iota(jnp.int32, (BLOCK_SIZE, BLOCK_SIZE), 1)

    def update(state, i, kb, vt, bias, mask):
        m, l, acc = state
        s = jnp.where(mask, _dot_nt(kb, qms[i]) + bias, NEG_INF)
        m_new = jnp.maximum(m, jnp.max(s, axis=0, keepdims=True))
        alpha = jnp.exp(m - m_new)
        p = jnp.exp(s - m_new)
        l_new = alpha * l + jnp.sum(p, axis=0, keepdims=True)
        acc_new = alpha * acc + _dot(vt, p.astype(BF16))
        return m_new, l_new, acc_new

    def key_block(j):
        return k_ref[0, pl.ds(pl.multiple_of(j * BLOCK_SIZE, BLOCK_SIZE), BLOCK_SIZE), :].astype(BF16)

    init = (jnp.full((1, BLOCK_SIZE), NEG_INF, F32), jnp.zeros((1, BLOCK_SIZE), F32),
            jnp.zeros((V7X_LANES, BLOCK_SIZE), F32))
    kb, vt = key_block(qb), vt_sc[qb]
    states = [update(init, i, kb, vt, bt_ref[i, 0], qq >= kk) for i in range(HEADS_PER_GROUP)]
    j1 = jnp.maximum(qb - 1, 0)
    kb, vt = key_block(j1), vt_sc[j1]
    states = [update(states[i], i, kb, vt, bt_ref[i, 1], sel_sc[i, pl.ds(j1, 1), :] > 0.5)
              for i in range(HEADS_PER_GROUP)]

    def far_body(j, states):
        kb, vt = key_block(j), vt_sc[j]
        return tuple(update(states[i], i, kb, vt, rb_ref[N_BUCKETS - 1, hg * HEADS_PER_GROUP + i],
                            sel_sc[i, pl.ds(j, 1), :] > 0.5) for i in range(HEADS_PER_GROUP))

    states = lax.fori_loop(0, jnp.maximum(qb - 1, 0), far_body, tuple(states))
    dim_row = lax.broadcasted_iota(jnp.int32, (V7X_LANES, BLOCK_SIZE), 0)
    out_t = jnp.zeros((V7X_LANES, BLOCK_SIZE), F32)
    for i in range(HEADS_PER_GROUP):
        m, l, acc = states[i]
        out_t = jnp.where(dim_row // HEAD_DIM == i, acc / l, out_t)
    o_ref[0] = out_t.T


def _attn_prompt(rel_bias, q, k, v, bias_tiles):
    batch, seq, d = q.shape
    n_blk = seq // BLOCK_SIZE
    n_hg = d // V7X_LANES
    q_spec = pl.BlockSpec((1, BLOCK_SIZE, V7X_LANES), lambda b, g, i: (b, i, g))
    kv_spec = pl.BlockSpec((1, seq, V7X_LANES), lambda b, g, i: (b, 0, g))
    return pl.pallas_call(
        _attn_prompt_kernel,
        grid=(batch, n_hg, n_blk),
        in_specs=[pl.BlockSpec(memory_space=pltpu.SMEM), q_spec, kv_spec, kv_spec,
                  pl.BlockSpec((HEADS_PER_GROUP, 2, BLOCK_SIZE, BLOCK_SIZE), lambda b, g, i: (g, 0, 0, 0))],
        out_specs=q_spec,
        out_shape=jax.ShapeDtypeStruct((batch, seq, d), F32),
        scratch_shapes=[pltpu.VMEM((n_blk, V7X_LANES), F32),
                        pltpu.VMEM((n_blk, V7X_LANES, BLOCK_SIZE), BF16),
                        pltpu.VMEM((HEADS_PER_GROUP, n_blk, BLOCK_SIZE), F32)],
        compiler_params=pltpu.CompilerParams(dimension_semantics=("arbitrary", "arbitrary", "arbitrary"),
                                             vmem_limit_bytes=V7X_VMEM_LIMIT_BYTES),
        name="attn_prompt",
    )(rel_bias, q, k, v, bias_tiles)


def _select_blocks_kernel(pt_ref, q_ref, ck_ref, o_ref, pages, score_sc, sem, *, layer, n_groups):
    b = pl.program_id(0)
    g = pl.program_id(1)
    step = b * n_groups + g
    n_steps = pl.num_programs(0) * n_groups
    page_size = ck_ref.shape[2]
    pages_per_block = BLOCK_SIZE // page_size
    blocks_per_step = SCAN_PAGES // pages_per_block
    shape = score_sc.shape
    lane = lax.broadcasted_iota(jnp.int32, shape, 1)

    def copies(step, slot):
        return [pltpu.make_async_copy(ck_ref.at[layer, pt_ref[step * SCAN_PAGES + i]], pages.at[slot, i],
                                      sem.at[slot]) for i in range(SCAN_PAGES)]

    slot = step % 2

    @pl.when(step == 0)
    def _():
        for c in copies(0, 0):
            c.start()

    @pl.when(step + 1 < n_steps)
    def _():
        for c in copies(step + 1, 1 - slot):
            c.start()

    for c in copies(step, slot):
        c.wait()

    @pl.when(g == 0)
    def _():
        score_sc[...] = jnp.full(shape, NEG_INF, F32)

    q = q_ref[0]
    scores = score_sc[...]
    for j in range(blocks_per_step):
        block_sum = jnp.sum(pages[slot, pages_per_block * j], axis=0)
        for p in range(1, pages_per_block):
            block_sum = block_sum + jnp.sum(pages[slot, pages_per_block * j + p], axis=0)
        s = jnp.sum(q * (block_sum / BLOCK_SIZE), axis=1, keepdims=True)
        scores = jnp.where(lane == g * blocks_per_step + j, s, scores)
    score_sc[...] = scores

    @pl.when(g == n_groups - 1)
    def _():
        cur = scores
        out = jnp.zeros(shape, jnp.int32)
        for t in range(TOP_K):
            best = jnp.max(cur, axis=1, keepdims=True)
            idx = jnp.min(jnp.where(cur == best, lane, V7X_LANES), axis=1, keepdims=True)
            out = jnp.where(lane == t, idx, out)
            cur = jnp.where(lane == idx, NEG_INF, cur)
        o_ref[0] = out


def _select_blocks(page_table_flat, q, cache_k, layer, n_pages):
    n_seq = q.shape[0]
    page_size = cache_k.shape[2]
    n_groups = n_pages // SCAN_PAGES
    seq_spec = lambda last: pl.BlockSpec((1, N_HEADS, last), lambda b, g, pt: (b, 0, 0))
    out = pl.pallas_call(
        functools.partial(_select_blocks_kernel, layer=layer, n_groups=n_groups),
        grid_spec=pltpu.PrefetchScalarGridSpec(
            num_scalar_prefetch=1,
            grid=(n_seq, n_groups),
            in_specs=[seq_spec(HEAD_DIM), pl.BlockSpec(memory_space=pl.ANY)],
            out_specs=seq_spec(V7X_LANES),
            scratch_shapes=[pltpu.VMEM((2, SCAN_PAGES, page_size, N_HEADS, HEAD_DIM), F32),
                            pltpu.VMEM((N_HEADS, V7X_LANES), F32),
                            pltpu.SemaphoreType.DMA((2,))]),
        out_shape=jax.ShapeDtypeStruct((n_seq, N_HEADS, V7X_LANES), jnp.int32),
        compiler_params=pltpu.CompilerParams(dimension_semantics=("arbitrary", "arbitrary"),
                                             vmem_limit_bytes=V7X_VMEM_LIMIT_BYTES),
        name="select_blocks",
    )(page_table_flat, q, cache_k)
    return out[:, :, :TOP_K]


def _attn_sample_kernel(pt_ref, sel_ref, rb_ref, q_ref, kn_ref, vn_ref, bl_ref, ck_ref, cv_ref, o_ref,
                        kbuf, vbuf, sem, *, layer, n_pages):
    b = pl.program_id(0)
    n_seq = pl.num_programs(0)
    page_size = ck_ref.shape[2]
    pages_per_block = BLOCK_SIZE // page_size
    n_rows = TOP_K * BLOCK_SIZE
    last_block = n_pages // pages_per_block - 1

    def selected(seq, h, s):
        return sel_ref[(seq * N_HEADS + h) * TOP_K + s]

    def copies(seq, slot):
        out = []
        for h in range(N_HEADS):
            for s in range(TOP_K):
                blk = selected(seq, h, s)
                for p in range(pages_per_block):
                    phys = pt_ref[seq * n_pages + blk * pages_per_block + p]
                    dst = pl.ds((s * pages_per_block + p) * page_size, page_size)
                    out.append(pltpu.make_async_copy(ck_ref.at[layer, phys, :, h, :],
                                                     kbuf.at[slot, h, dst, :], sem.at[0, slot]))
                    out.append(pltpu.make_async_copy(cv_ref.at[layer, phys, :, h, :],
                                                     vbuf.at[slot, h, dst, :], sem.at[1, slot]))
        return out

    slot = b % 2

    @pl.when(b == 0)
    def _():
        for c in copies(0, 0):
            c.start()

    @pl.when(b + 1 < n_seq)
    def _():
        for c in copies(b + 1, 1 - slot):
            c.start()

    for c in copies(b, slot):
        c.wait()

    for h in range(N_HEADS):
        qh = q_ref[0, h:h + 1, :]
        far = rb_ref[N_BUCKETS - 1, h]
        bias = jnp.concatenate(
            [jnp.where(selected(b, h, s) == last_block, bl_ref[:, h:h + 1], far) for s in range(TOP_K)], axis=0)
        logit = jnp.sum(kbuf[slot, h] * qh, axis=1, keepdims=True) * SCALE + bias
        logit_new = jnp.sum(kn_ref[0, h:h + 1, :] * qh, axis=1, keepdims=True) * SCALE + rb_ref[0, h]
        m = jnp.maximum(jnp.max(logit, axis=0, keepdims=True), logit_new)
        p = jnp.exp(logit - m)
        p_new = jnp.exp(logit_new - m)
        denom = jnp.sum(p, axis=0, keepdims=True) + p_new
        num = jnp.sum(p * vbuf[slot, h], axis=0, keepdims=True) + p_new * vn_ref[0, h:h + 1, :]
        o_ref[0, h:h + 1, :] = num / denom


def _attn_sample(page_table_flat, sel_flat, rel_bias, q, k_new, v_new, bias_last, cache_k, cache_v, layer,
                 n_pages):
    n_seq = q.shape[0]
    seq_spec = pl.BlockSpec((1, N_HEADS, HEAD_DIM), lambda b, pt, sel: (b, 0, 0))
    n_rows = TOP_K * BLOCK_SIZE
    return pl.pallas_call(
        functools.partial(_attn_sample_kernel, layer=layer, n_pages=n_pages),
        grid_spec=pltpu.PrefetchScalarGridSpec(
            num_scalar_prefetch=2,
            grid=(n_seq,),
            in_specs=[pl.BlockSpec(memory_space=pltpu.SMEM), seq_spec, seq_spec, seq_spec,
                      pl.BlockSpec((BLOCK_SIZE, N_HEADS), lambda b, pt, sel: (0, 0)),
                      pl.BlockSpec(memory_space=pl.ANY), pl.BlockSpec(memory_space=pl.ANY)],
            out_specs=seq_spec,
            scratch_shapes=[pltpu.VMEM((2, N_HEADS, n_rows, HEAD_DIM), F32),
                            pltpu.VMEM((2, N_HEADS, n_rows, HEAD_DIM), F32),
                            pltpu.SemaphoreType.DMA((2, 2))]),
        out_shape=jax.ShapeDtypeStruct((n_seq, N_HEADS, HEAD_DIM), F32),
        compiler_params=pltpu.CompilerParams(dimension_semantics=("arbitrary",),
                                             vmem_limit_bytes=V7X_VMEM_LIMIT_BYTES),
        name="attn_sample",
    )(page_table_flat, sel_flat, rel_bias, q, k_new, v_new, bias_last, cache_k, cache_v)


def _mix_out_kernel(x_ref, oa_ref, ga_ref, zc_ref, wba_ref, wout_ref, g2_ref, wg_ref, wu_ref, wd_ref, gf_ref,
                    y_ref, *, final):
    y_attn = _dot(oa_ref[...].astype(BF16), wba_ref[...])
    merged = ga_ref[...] * y_attn + zc_ref[...]
    x = x_ref[...] + _dot(merged.astype(BF16), wout_ref[...])
    h = _rms(x, g2_ref[...]).astype(BF16)
    x = x + 0.5 * _swiglu(h, wg_ref, wu_ref, wd_ref)
    y_ref[...] = _rms(x, gf_ref[...]) if final else x


def _mix_out(x, oa, ga, zc, wba, wout, g2, wg, wu, wd, gf, rows, final):
    n, d = x.shape
    row_spec = pl.BlockSpec((rows, d), lambda i: (i, 0))
    return pl.pallas_call(
        functools.partial(_mix_out_kernel, final=final),
        grid=(n // rows,),
        in_specs=[row_spec] * 4 + [_const_spec(wba.shape), _const_spec(wout.shape), _const_spec((1, d)),
                                   _const_spec(wg.shape), _const_spec(wu.shape), _const_spec(wd.shape),
                                   _const_spec((1, d))],
        out_specs=row_spec,
        out_shape=jax.ShapeDtypeStruct((n, d), F32),
        compiler_params=pltpu.CompilerParams(dimension_semantics=("arbitrary",),
                                             vmem_limit_bytes=V7X_VMEM_LIMIT_BYTES),
        name="mix_out",
    )(x, oa, ga, zc, wba, wout, g2, wg, wu, wd, gf)


def kernel(x_prompt, x_sample, cache_k, cache_v, state_conv, page_table, rel_bias, norm_ffn1, w_ffn1_gate,
           w_ffn1_up, w_ffn1_down, norm_mix, w_in, w_conv, w_branch_attn, w_branch_conv, w_out, norm_ffn2,
           w_ffn2_gate, w_ffn2_up, w_ffn2_down, norm_final):
    batch, seq, d = x_prompt.shape
    n_seq, t_new, _ = x_sample.shape
    depth = w_in.shape[0]
    n_pages = page_table.shape[1]
    page_size = cache_k.shape[2]
    assert t_new == 1 and d == N_HEADS * HEAD_DIM and w_in.shape[2] == 8 * d
    assert seq % PROMPT_ROWS == 0 and PROMPT_ROWS % BLOCK_SIZE == 0
    assert (n_pages * page_size) % BLOCK_SIZE == 0 and n_pages % SCAN_PAGES == 0 and BLOCK_SIZE % page_size == 0
    assert MAX_DISTANCE <= BLOCK_SIZE

    xp = x_prompt.reshape(batch * seq, d)
    xs = x_sample.reshape(n_seq, d)
    pt_flat = page_table.reshape(-1)
    bias_tiles = _bias_tiles(rel_bias)
    bias_last = _bias_last_block(rel_bias)
    outs = {name: [] for name in ("kp", "vp", "cp", "ks", "vs", "cs")}
    for layer in range(depth):
        g1, gm, g2 = (g[layer].reshape(1, d) for g in (norm_ffn1, norm_mix, norm_ffn2))
        gf = norm_final.reshape(1, d)
        wg1, wu1, wd1, win, wba, wbc, wout, wg2, wu2, wd2 = (
            w[layer].astype(BF16) for w in (w_ffn1_gate, w_ffn1_up, w_ffn1_down, w_in, w_branch_attn,
                                            w_branch_conv, w_out, w_ffn2_gate, w_ffn2_up, w_ffn2_down))
        wconv = w_conv[layer]
        final = layer == depth - 1

        xp = _ffn(xp, g1, wg1, wu1, wd1, PROMPT_ROWS)
        q, k, v, ga, zc, conv_state = _mix_in_prompt(xp, gm, win, wconv, wbc, batch, seq, PROMPT_ROWS)
        oa = _attn_prompt(rel_bias, q.reshape(batch, seq, d), k.reshape(batch, seq, d),
                          v.reshape(batch, seq, d), bias_tiles).reshape(batch * seq, d)
        xp = _mix_out(xp, oa, ga, zc, wba, wout, g2, wg2, wu2, wd2, gf, PROMPT_ROWS, final)
        outs["kp"].append(k.reshape(batch, seq, N_HEADS, HEAD_DIM))
        outs["vp"].append(v.reshape(batch, seq, N_HEADS, HEAD_DIM))
        outs["cp"].append(conv_state)

        xs = _ffn(xs, g1, wg1, wu1, wd1, n_seq)
        prev2, prev1 = state_conv[layer, :, 0, :], state_conv[layer, :, 1, :]
        q, k, v, ga, zc, cu = _mix_in_sample(xs, gm, win, wconv, wbc, prev2, prev1)
        q3, k3, v3 = (a.reshape(n_seq, N_HEADS, HEAD_DIM) for a in (q, k, v))
        sel = _select_blocks(pt_flat, q3, cache_k, layer, n_pages)
        oa = _attn_sample(pt_flat, sel.reshape(-1), rel_bias, q3, k3, v3, bias_last, cache_k, cache_v, layer,
                          n_pages).reshape(n_seq, d)
        xs = _mix_out(xs, oa, ga, zc, wba, wout, g2, wg2, wu2, wd2, gf, n_seq, final)
        outs["ks"].append(k3.reshape(n_seq, 1, N_HEADS, HEAD_DIM))
        outs["vs"].append(v3.reshape(n_seq, 1, N_HEADS, HEAD_DIM))
        outs["cs"].append(jnp.stack([prev1, cu], axis=1))

    return (xp.reshape(batch, seq, d), xs.reshape(n_seq, 1, d), jnp.stack(outs["kp"]), jnp.stack(outs["vp"]),
            jnp.stack(outs["cp"]), jnp.stack(outs["ks"]), jnp.stack(outs["vs"]), jnp.stack(outs["cs"]))
```

```python
import functools
import math

import jax
import jax.numpy as jnp
from jax import lax
from jax.experimental import pallas as pl
from jax.experimental.pallas import tpu as pltpu

N_HEADS = 16
HEAD_DIM = 64
BLOCK_SIZE = 256
TOP_K = 3
N_BUCKETS = 32
MAX_DISTANCE = 128
CONV_WIDTH = 3
RMS_EPS = 1e-6
SCALE = HEAD_DIM ** -0.5

V7X_LANES = 128
V7X_SUBLANES = 8
V7X_VMEM_LIMIT_BYTES = 56 * 1024 * 1024

HEADS_PER_GROUP = V7X_LANES // HEAD_DIM
FF_CHUNK = 256
PROMPT_ROWS = 512
ATTN_HEADS = 4
SCAN_PAGES = 16

F32 = jnp.float32
BF16 = jnp.bfloat16
NEG_INF = float("-inf")


def _const_spec(shape):
    nd = len(shape)
    return pl.BlockSpec(shape, lambda *_: (0,) * nd, pipeline_mode=pl.Buffered(1))


def _rms(x, g):
    ms = jnp.mean(x * x, axis=-1, keepdims=True)
    return (x * lax.rsqrt(ms + RMS_EPS)) * g


def _dot(a, b):
    return jnp.dot(a, b, preferred_element_type=F32)


def _dot_nt(a, b):
    return lax.dot_general(a, b, (((1,), (1,)), ((), ())), preferred_element_type=F32)


def _swiglu(h, wg_ref, wu_ref, wd_ref):
    d_ff = wg_ref.shape[1]
    acc = None
    for c in range(d_ff // FF_CHUNK):
        sl = slice(c * FF_CHUNK, (c + 1) * FF_CHUNK)
        gate = _dot(h, wg_ref[:, sl])
        up = _dot(h, wu_ref[:, sl])
        act = (gate * jax.nn.sigmoid(gate) * up).astype(BF16)
        part = _dot(act, wd_ref[sl, :])
        acc = part if acc is None else acc + part
    return acc


def _ffn_kernel(x_ref, g_ref, wg_ref, wu_ref, wd_ref, o_ref):
    x = x_ref[...]
    h = _rms(x, g_ref[...]).astype(BF16)
    o_ref[...] = x + 0.5 * _swiglu(h, wg_ref, wu_ref, wd_ref)


def _ffn(x, g, wg, wu, wd, rows):
    n, d = x.shape
    d_ff = wg.shape[1]
    row_spec = pl.BlockSpec((rows, d), lambda i: (i, 0))
    return pl.pallas_call(
        _ffn_kernel,
        grid=(n // rows,),
        in_specs=[row_spec, _const_spec((1, d)), _const_spec((d, d_ff)), _const_spec((d, d_ff)),
                  _const_spec((d_ff, d))],
        out_specs=row_spec,
        out_shape=jax.ShapeDtypeStruct((n, d), F32),
        compiler_params=pltpu.CompilerParams(dimension_semantics=("arbitrary",),
                                             vmem_limit_bytes=V7X_VMEM_LIMIT_BYTES),
        name="ffn",
    )(x, g, wg, wu, wd)


def _proj(h, win_ref, n):
    d = h.shape[1]
    return _dot(h, win_ref[:, n * d:(n + 1) * d])


def _mix_in_tail(h, win_ref, wbc_ref, gate_b, conv, ga_ref, zc_ref):
    y_conv = _dot((gate_b * conv).astype(BF16), wbc_ref[...])
    ga_ref[...] = jax.nn.sigmoid(_proj(h, win_ref, 6))
    zc_ref[...] = jax.nn.sigmoid(_proj(h, win_ref, 7)) * y_conv


def _mix_in_prompt_kernel(x_ref, g_ref, win_ref, wconv_ref, wbc_ref,
                          q_ref, k_ref, v_ref, ga_ref, zc_ref, cs_ref, cbuf, *, tiles_per_seq):
    i = pl.program_id(0)
    rows, d = x_ref.shape
    h = _rms(x_ref[...], g_ref[...]).astype(BF16)
    q_ref[...] = _proj(h, win_ref, 0)
    k_ref[...] = _proj(h, win_ref, 1)
    v_ref[...] = _proj(h, win_ref, 2)
    gate_b = _proj(h, win_ref, 3)
    cu = _proj(h, win_ref, 4) * _proj(h, win_ref, 5)

    @pl.when(i % tiles_per_seq == 0)
    def _():
        cbuf[0:V7X_SUBLANES, :] = jnp.zeros((V7X_SUBLANES, d), F32)

    cbuf[V7X_SUBLANES:V7X_SUBLANES + rows, :] = cu
    w = wconv_ref[...]
    conv = (w[0:1] * cbuf[V7X_SUBLANES - 2:V7X_SUBLANES - 2 + rows, :]
            + w[1:2] * cbuf[V7X_SUBLANES - 1:V7X_SUBLANES - 1 + rows, :]
            + w[2:3] * cu)
    last = cbuf[rows + V7X_SUBLANES - 2:rows + V7X_SUBLANES, :]
    cs_ref[0] = last
    cbuf[V7X_SUBLANES - 2:V7X_SUBLANES, :] = last
    _mix_in_tail(h, win_ref, wbc_ref, gate_b, conv, ga_ref, zc_ref)


def _mix_in_sample_kernel(x_ref, g_ref, win_ref, wconv_ref, wbc_ref, p2_ref, p1_ref,
                          q_ref, k_ref, v_ref, ga_ref, zc_ref, cu_ref):
    h = _rms(x_ref[...], g_ref[...]).astype(BF16)
    q_ref[...] = _proj(h, win_ref, 0)
    k_ref[...] = _proj(h, win_ref, 1)
    v_ref[...] = _proj(h, win_ref, 2)
    gate_b = _proj(h, win_ref, 3)
    cu = _proj(h, win_ref, 4) * _proj(h, win_ref, 5)
    cu_ref[...] = cu
    w = wconv_ref[...]
    conv = w[0:1] * p2_ref[...] + w[1:2] * p1_ref[...] + w[2:3] * cu
    _mix_in_tail(h, win_ref, wbc_ref, gate_b, conv, ga_ref, zc_ref)


def _mix_in_prompt(x, g, win, wconv, wbc, batch, seq, rows):
    n, d = x.shape
    tiles_per_seq = seq // rows
    row_spec = pl.BlockSpec((rows, d), lambda i: (i, 0))
    row_shape = jax.ShapeDtypeStruct((n, d), F32)
    return pl.pallas_call(
        functools.partial(_mix_in_prompt_kernel, tiles_per_seq=tiles_per_seq),
        grid=(n // rows,),
        in_specs=[row_spec, _const_spec((1, d)), _const_spec(win.shape), _const_spec(wconv.shape),
                  _const_spec(wbc.shape)],
        out_specs=[row_spec] * 5 + [pl.BlockSpec((1, CONV_WIDTH - 1, d), lambda i: (i // tiles_per_seq, 0, 0))],
        out_shape=[row_shape] * 5 + [jax.ShapeDtypeStruct((batch, CONV_WIDTH - 1, d), F32)],
        scratch_shapes=[pltpu.VMEM((rows + V7X_SUBLANES, d), F32)],
        compiler_params=pltpu.CompilerParams(dimension_semantics=("arbitrary",),
                                             vmem_limit_bytes=V7X_VMEM_LIMIT_BYTES),
        name="mix_in_prompt",
    )(x, g, win, wconv, wbc)


def _mix_in_sample(x, g, win, wconv, wbc, prev2, prev1):
    n, d = x.shape
    full = pl.BlockSpec((n, d), lambda i: (0, 0))
    row_shape = jax.ShapeDtypeStruct((n, d), F32)
    return pl.pallas_call(
        _mix_in_sample_kernel,
        grid=(1,),
        in_specs=[full, _const_spec((1, d)), _const_spec(win.shape), _const_spec(wconv.shape),
                  _const_spec(wbc.shape), full, full],
        out_specs=[full] * 6,
        out_shape=[row_shape] * 6,
        compiler_params=pltpu.CompilerParams(dimension_semantics=("arbitrary",),
                                             vmem_limit_bytes=V7X_VMEM_LIMIT_BYTES),
        name="mix_in_sample",
    )(x, g, win, wconv, wbc, prev2, prev1)


def _t5_bucket(dist):
    n = jnp.maximum(dist, 0)
    max_exact = N_BUCKETS // 2
    nf = jnp.maximum(n, 1).astype(F32)
    large = max_exact + (jnp.log(nf / max_exact) / math.log(MAX_DISTANCE / max_exact)
                         * (N_BUCKETS - max_exact)).astype(jnp.int32)
    return jnp.where(n < max_exact, n, jnp.minimum(large, N_BUCKETS - 1))


def _bias_lookup(dist, table_row):
    bucket = _t5_bucket(dist)
    bias = jnp.zeros(dist.shape, F32)
    for b in range(N_BUCKETS):
        bias = jnp.where(bucket == b, table_row(b), bias)
    return bias


def _bias_tiles_kernel(rb_ref, o_ref):
    h = pl.program_id(0)
    kk = lax.broadcasted_iota(jnp.int32, (BLOCK_SIZE, BLOCK_SIZE), 0)
    qq = lax.broadcasted_iota(jnp.int32, (BLOCK_SIZE, BLOCK_SIZE), 1)
    own = _bias_lookup(qq - kk, lambda b: rb_ref[b, h])
    o_ref[0, 0] = jnp.where(qq >= kk, own, NEG_INF)
    o_ref[0, 1] = _bias_lookup(qq - kk + BLOCK_SIZE, lambda b: rb_ref[b, h])


def _bias_tiles(rel_bias):
    return pl.pallas_call(
        _bias_tiles_kernel,
        grid=(N_HEADS,),
        in_specs=[pl.BlockSpec(memory_space=pltpu.SMEM)],
        out_specs=pl.BlockSpec((1, 2, BLOCK_SIZE, BLOCK_SIZE), lambda h: (h, 0, 0, 0)),
        out_shape=jax.ShapeDtypeStruct((N_HEADS, 2, BLOCK_SIZE, BLOCK_SIZE), F32),
        name="bias_tiles",
    )(rel_bias)


def _bias_last_block_kernel(rb_ref, o_ref):
    h = pl.program_id(0)
    _, n_p, page_size = o_ref.shape
    p = lax.broadcasted_iota(jnp.int32, (n_p, page_size), 0)
    r = lax.broadcasted_iota(jnp.int32, (n_p, page_size), 1)
    o_ref[0] = _bias_lookup(BLOCK_SIZE - (p * page_size + r), lambda b: rb_ref[b, h])


def _bias_last_block(rel_bias, page_size):
    shape = (N_HEADS, BLOCK_SIZE // page_size, page_size)
    return pl.pallas_call(
        _bias_last_block_kernel,
        grid=(N_HEADS,),
        in_specs=[pl.BlockSpec(memory_space=pltpu.SMEM)],
        out_specs=pl.BlockSpec((1,) + shape[1:], lambda h: (h, 0, 0)),
        out_shape=jax.ShapeDtypeStruct(shape, F32),
        name="bias_last_block",
    )(rel_bias)


def _attn_prompt_kernel(rb_ref, q_ref, k_ref, v_ref, bt_ref, o_ref, kmean_sc, kb_sc, vt_sc, sel_sc):
    hq = pl.program_id(1)
    qb = pl.program_id(2)
    n_blk = k_ref.shape[1] // BLOCK_SIZE
    n_grp = q_ref.shape[2] // V7X_LANES
    n_heads = n_grp * HEADS_PER_GROUP
    bs = BLOCK_SIZE

    @pl.when(qb == 0)
    def _():
        kb_sc[0:bs, :] = jnp.zeros((bs, kb_sc.shape[1]), BF16)
        kb_sc[bs:, :] = k_ref[0].astype(BF16)
        sel_sc[:, n_blk:, :] = jnp.zeros((n_heads, sel_sc.shape[1] - n_blk, bs), F32)
        means = []
        for j in range(n_blk):
            rows = slice(j * bs, (j + 1) * bs)
            means.append(jnp.mean(k_ref[0, rows, :], axis=0, keepdims=True))
            for g in range(n_grp):
                vt_sc[g, j + 1] = v_ref[0, rows, g * V7X_LANES:(g + 1) * V7X_LANES].T.astype(BF16)
        for g in range(n_grp):
            vt_sc[g, 0] = jnp.zeros((V7X_LANES, bs), BF16)
        kmean_sc[...] = jnp.concatenate(means, axis=0)

    q = q_ref[0] * SCALE
    lane = lax.broadcasted_iota(jnp.int32, (bs, V7X_LANES), 1)
    blk_row = lax.broadcasted_iota(jnp.int32, (n_blk, bs), 0)
    qms = []
    for c in range(n_heads):
        g, i = divmod(c, HEADS_PER_GROUP)
        lanes = slice(g * V7X_LANES, (g + 1) * V7X_LANES)
        qm = jnp.where(lane // HEAD_DIM == i, q[:, lanes], 0.0).astype(BF16)
        qms.append(qm)
        st = _dot_nt(kmean_sc[:, lanes].astype(BF16), qm)
        rank = jnp.zeros(st.shape, jnp.int32)
        for j in range(n_blk):
            sj = st[j:j + 1, :]
            beats = (sj > st) | ((sj == st) & (j < blk_row))
            rank = rank + jnp.where(beats & (j < qb), 1, 0)
        sel_sc[c, 0:n_blk, :] = jnp.where((rank < TOP_K) & (blk_row < qb), 1.0, 0.0)

    def col_max(x):
        return jnp.max(x, axis=0, keepdims=True)

    def col_sum(x):
        return jnp.sum(x, axis=0, keepdims=True)

    def window(c, first_padded_block):
        g = c // HEADS_PER_GROUP
        rows = pl.ds(pl.multiple_of(first_padded_block * bs, bs), 2 * bs)
        s = _dot_nt(kb_sc[rows, g * V7X_LANES:(g + 1) * V7X_LANES], qms[c])
        return s[:bs], s[bs:]

    prev_row = jnp.where(qb == 0, n_blk, qb - 1)
    states = []
    for c in range(n_heads):
        g = c // HEADS_PER_GROUP
        s_prev, s_own = window(c, qb)
        s_prev = s_prev + bt_ref[c, 1] + jnp.where(sel_sc[c, pl.ds(prev_row, 1), :] > 0.5, 0.0, NEG_INF)
        s_own = s_own + bt_ref[c, 0]
        m = jnp.maximum(col_max(s_prev), col_max(s_own))
        p_prev, p_own = jnp.exp(s_prev - m), jnp.exp(s_own - m)
        l = col_sum(p_prev) + col_sum(p_own)
        acc = _dot(vt_sc[g, qb], p_prev.astype(BF16)) + _dot(vt_sc[g, qb + 1], p_own.astype(BF16))
        states.append((m, l, acc))

    n_far = jnp.maximum(qb - 1, 0)

    def far_body(t, states):
        j0 = 2 * t
        new = []
        for c in range(n_heads):
            g = c // HEADS_PER_GROUP
            far_bias = rb_ref[N_BUCKETS - 1, hq * n_heads + c]
            s0, s1 = window(c, j0 + 1)
            s0 = s0 + jnp.where(sel_sc[c, pl.ds(j0, 1), :] > 0.5, far_bias, NEG_INF)
            s1 = s1 + jnp.where((sel_sc[c, pl.ds(j0 + 1, 1), :] > 0.5) & (j0 + 1 < n_far), far_bias, NEG_INF)
            m, l, acc = states[c]
            m_new = jnp.maximum(m, jnp.maximum(col_max(s0), col_max(s1)))
            alpha = jnp.exp(m - m_new)
            p0, p1 = jnp.exp(s0 - m_new), jnp.exp(s1 - m_new)
            l_new = alpha * l + col_sum(p0) + col_sum(p1)
            acc_new = (alpha * acc + _dot(vt_sc[g, j0 + 1], p0.astype(BF16))
                       + _dot(vt_sc[g, j0 + 2], p1.astype(BF16)))
            new.append((m_new, l_new, acc_new))
        return tuple(new)

    states = lax.fori_loop(0, (n_far + 1) // 2, far_body, tuple(states))
    dim_row = lax.broadcasted_iota(jnp.int32, (V7X_LANES, bs), 0)
    for g in range(n_grp):
        out_t = jnp.zeros((V7X_LANES, bs), F32)
        for i in range(HEADS_PER_GROUP):
            m, l, acc = states[g * HEADS_PER_GROUP + i]
            out_t = jnp.where(dim_row // HEAD_DIM == i, acc * (1.0 / l), out_t)
        o_ref[0, :, g * V7X_LANES:(g + 1) * V7X_LANES] = out_t.T


def _attn_prompt(rel_bias, q, k, v, bias_tiles):
    batch, seq, d = q.shape
    n_blk = seq // BLOCK_SIZE
    width = ATTN_HEADS * HEAD_DIM
    n_grp = width // V7X_LANES
    q_spec = pl.BlockSpec((1, BLOCK_SIZE, width), lambda b, g, i: (b, i, g))
    kv_spec = pl.BlockSpec((1, seq, width), lambda b, g, i: (b, 0, g))
    return pl.pallas_call(
        _attn_prompt_kernel,
        grid=(batch, d // width, n_blk),
        in_specs=[pl.BlockSpec(memory_space=pltpu.SMEM), q_spec, kv_spec, kv_spec,
                  pl.BlockSpec((ATTN_HEADS, 2, BLOCK_SIZE, BLOCK_SIZE), lambda b, g, i: (g, 0, 0, 0))],
        out_specs=q_spec,
        out_shape=jax.ShapeDtypeStruct((batch, seq, d), F32),
        scratch_shapes=[pltpu.VMEM((n_blk, width), F32),
                        pltpu.VMEM((seq + BLOCK_SIZE, width), BF16),
                        pltpu.VMEM((n_grp, n_blk + 1, V7X_LANES, BLOCK_SIZE), BF16),
                        pltpu.VMEM((ATTN_HEADS, 2 * V7X_SUBLANES, BLOCK_SIZE), F32)],
        compiler_params=pltpu.CompilerParams(dimension_semantics=("arbitrary", "arbitrary", "arbitrary"),
                                             vmem_limit_bytes=V7X_VMEM_LIMIT_BYTES),
        name="attn_prompt",
    )(rel_bias, q, k, v, bias_tiles)


def _scan_keys_kernel(pt_ref, qt_ref, ck_ref, lg_ref, sel_ref, pages, qb_sc, pg_sc, sem, *, layer, n_groups):
    b = pl.program_id(0)
    g = pl.program_id(1)
    step = b * n_groups + g
    n_steps = pl.num_programs(0) * n_groups
    page_size = ck_ref.shape[-1]
    pages_per_block = BLOCK_SIZE // page_size
    n_blocks = n_groups * SCAN_PAGES // pages_per_block

    def copies(step, slot):
        return [pltpu.make_async_copy(ck_ref.at[layer, pt_ref[step * SCAN_PAGES + i]], pages.at[slot, i],
                                      sem.at[slot]) for i in range(SCAN_PAGES)]

    slot = step % 2

    @pl.when(step == 0)
    def _():
        for c in copies(0, 0):
            c.start()

    @pl.when(step + 1 < n_steps)
    def _():
        for c in copies(step + 1, 1 - slot):
            c.start()

    @pl.when(g == 0)
    def _():
        qt = qt_ref[0]
        for h in range(N_HEADS):
            qb_sc[h] = jnp.broadcast_to(qt[:, h:h + 1], (HEAD_DIM, page_size))

    for c in copies(step, slot):
        c.wait()

    def page_body(i, carry):
        for h in range(N_HEADS):
            lg_ref[0, h, pl.ds(i, 1), :] = jnp.sum(pages[slot, i, h] * qb_sc[h], axis=0, keepdims=True)
        return carry

    lax.fori_loop(0, SCAN_PAGES, page_body, 0)

    lane = lax.broadcasted_iota(jnp.int32, (SCAN_PAGES, V7X_LANES), 1)
    tile = jnp.zeros((SCAN_PAGES, V7X_LANES), F32)
    for h in range(N_HEADS):
        tile = jnp.where(lane == h, jnp.sum(lg_ref[0, h], axis=1, keepdims=True), tile)
    pg_sc[pl.ds(pl.multiple_of(g * SCAN_PAGES, SCAN_PAGES), SCAN_PAGES), :] = tile

    @pl.when(g == n_groups - 1)
    def _():
        cur = pg_sc[pl.ds(0, n_blocks, stride=pages_per_block), :]
        for p in range(1, pages_per_block):
            cur = cur + pg_sc[pl.ds(p, n_blocks, stride=pages_per_block), :]
        cur = cur / BLOCK_SIZE
        row = lax.broadcasted_iota(jnp.int32, cur.shape, 0)
        out_row = lax.broadcasted_iota(jnp.int32, sel_ref.shape[1:], 0)
        out = jnp.zeros(sel_ref.shape[1:], jnp.int32)
        for t in range(TOP_K):
            best = jnp.max(cur, axis=0, keepdims=True)
            idx = jnp.min(jnp.where(cur == best, row, n_blocks), axis=0, keepdims=True)
            out = jnp.where(out_row == t, idx, out)
            cur = jnp.where(row == idx, NEG_INF, cur)
        sel_ref[0] = out


def _scan_keys(page_table_flat, q_t, cache_k_t, layer, n_pages):
    n_seq = q_t.shape[0]
    page_size = cache_k_t.shape[-1]
    n_groups = n_pages // SCAN_PAGES
    logits, sel = pl.pallas_call(
        functools.partial(_scan_keys_kernel, layer=layer, n_groups=n_groups),
        grid_spec=pltpu.PrefetchScalarGridSpec(
            num_scalar_prefetch=1,
            grid=(n_seq, n_groups),
            in_specs=[pl.BlockSpec((1, HEAD_DIM, N_HEADS), lambda b, g, pt: (b, 0, 0)),
                      pl.BlockSpec(memory_space=pl.ANY)],
            out_specs=[pl.BlockSpec((1, N_HEADS, SCAN_PAGES, page_size), lambda b, g, pt: (b, 0, g, 0)),
                       pl.BlockSpec((1, V7X_SUBLANES, V7X_LANES), lambda b, g, pt: (b, 0, 0))],
            scratch_shapes=[pltpu.VMEM((2, SCAN_PAGES, N_HEADS, HEAD_DIM, page_size), F32),
                            pltpu.VMEM((N_HEADS, HEAD_DIM, page_size), F32),
                            pltpu.VMEM((n_pages, V7X_LANES), F32),
                            pltpu.SemaphoreType.DMA((2,))]),
        out_shape=[jax.ShapeDtypeStruct((n_seq, N_HEADS, n_pages, page_size), F32),
                   jax.ShapeDtypeStruct((n_seq, V7X_SUBLANES, V7X_LANES), jnp.int32)],
        compiler_params=pltpu.CompilerParams(dimension_semantics=("arbitrary", "arbitrary"),
                                             vmem_limit_bytes=V7X_VMEM_LIMIT_BYTES),
        name="scan_keys",
    )(page_table_flat, q_t, cache_k_t)
    return logits, jnp.transpose(sel[:, :TOP_K, :N_HEADS], (0, 2, 1))


def _attn_sample_kernel(pt_ref, sel_ref, rb_ref, lg_ref, q_ref, kn_ref, vnt_ref, bl_ref, cv_ref, ot_ref,
                        vbuf, sem, *, layer, n_pages):
    b = pl.program_id(0)
    n_seq = pl.num_programs(0)
    page_size = cv_ref.shape[-1]
    pages_per_block = BLOCK_SIZE // page_size
    last_block = n_pages // pages_per_block - 1

    def selected(seq, h, s):
        return sel_ref[(seq * N_HEADS + h) * TOP_K + s]

    def copies(seq, slot):
        out = []
        for h in range(N_HEADS):
            for s in range(TOP_K):
                blk = selected(seq, h, s)
                for p in range(pages_per_block):
                    phys = pt_ref[seq * n_pages + blk * pages_per_block + p]
                    out.append(pltpu.make_async_copy(cv_ref.at[layer, phys, h],
                                                     vbuf.at[slot, h, s * pages_per_block + p], sem.at[slot]))
        return out

    slot = b % 2

    @pl.when(b == 0)
    def _():
        for c in copies(0, 0):
            c.start()

    @pl.when(b + 1 < n_seq)
    def _():
        for c in copies(b + 1, 1 - slot):
            c.start()

    for c in copies(b, slot):
        c.wait()

    def all_max(x):
        return jnp.max(jnp.max(x, axis=1, keepdims=True), axis=0, keepdims=True)

    def all_sum(x):
        return jnp.sum(jnp.sum(x, axis=1, keepdims=True), axis=0, keepdims=True)

    new_dots = jnp.sum(q_ref[0] * kn_ref[0], axis=1, keepdims=True) * SCALE
    for h in range(N_HEADS):
        logit_new = new_dots[h:h + 1, :] + rb_ref[0, h]
        pieces = []
        for s in range(TOP_K):
            blk = selected(b, h, s)
            raw = lg_ref[0, h, pl.ds(pl.multiple_of(blk * pages_per_block, pages_per_block), pages_per_block), :]
            bias = jnp.where(blk == last_block, bl_ref[h], rb_ref[N_BUCKETS - 1, h])
            pieces.append(raw * SCALE + bias)
        m = logit_new
        for x in pieces:
            m = jnp.maximum(m, all_max(x))
        p_new = jnp.exp(logit_new - m)
        denom = p_new
        acc = None
        for s, x in enumerate(pieces):
            p = jnp.exp(x - m)
            denom = denom + all_sum(p)
            for pg in range(pages_per_block):
                term = p[pg:pg + 1, :] * vbuf[slot, h, s * pages_per_block + pg]
                acc = term if acc is None else acc + term
        num = jnp.sum(acc, axis=1, keepdims=True) + p_new * vnt_ref[0, :, h:h + 1]
        ot_ref[0, :, h:h + 1] = num / denom


def _attn_sample(page_table_flat, sel_flat, rel_bias, logits, q, k_new, v_new_t, bias_last, cache_v_t, layer,
                 n_pages):
    n_seq = q.shape[0]
    page_size = cache_v_t.shape[-1]
    pages_per_block = BLOCK_SIZE // page_size
    seq_spec = pl.BlockSpec((1, N_HEADS, HEAD_DIM), lambda b, pt, sel: (b, 0, 0))
    seq_t_spec = pl.BlockSpec((1, HEAD_DIM, N_HEADS), lambda b, pt, sel: (b, 0, 0))
    return pl.pallas_call(
        functools.partial(_attn_sample_kernel, layer=layer, n_pages=n_pages),
        grid_spec=pltpu.PrefetchScalarGridSpec(
            num_scalar_prefetch=2,
            grid=(n_seq,),
            in_specs=[pl.BlockSpec(memory_space=pltpu.SMEM),
                      pl.BlockSpec((1, N_HEADS, n_pages, page_size), lambda b, pt, sel: (b, 0, 0, 0)),
                      seq_spec, seq_spec, seq_t_spec,
                      pl.BlockSpec(bias_last.shape, lambda b, pt, sel: (0, 0, 0)),
                      pl.BlockSpec(memory_space=pl.ANY)],
            out_specs=seq_t_spec,
            scratch_shapes=[pltpu.VMEM((2, N_HEADS, TOP_K * pages_per_block, HEAD_DIM, page_size), F32),
                            pltpu.SemaphoreType.DMA((2,))]),
        out_shape=jax.ShapeDtypeStruct((n_seq, HEAD_DIM, N_HEADS), F32),
        compiler_params=pltpu.CompilerParams(dimension_semantics=("arbitrary",),
                                             vmem_limit_bytes=V7X_VMEM_LIMIT_BYTES),
        name="attn_sample",
    )(page_table_flat, sel_flat, rel_bias, logits, q, k_new, v_new_t, bias_last, cache_v_t)


def _mix_out_kernel(x_ref, oa_ref, ga_ref, zc_ref, wba_ref, wout_ref, g2_ref, wg_ref, wu_ref, wd_ref, gf_ref,
                    y_ref, *, final):
    y_attn = _dot(oa_ref[...].astype(BF16), wba_ref[...])
    merged = ga_ref[...] * y_attn + zc_ref[...]
    x = x_ref[...] + _dot(merged.astype(BF16), wout_ref[...])
    h = _rms(x, g2_ref[...]).astype(BF16)
    x = x + 0.5 * _swiglu(h, wg_ref, wu_ref, wd_ref)
    y_ref[...] = _rms(x, gf_ref[...]) if final else x


def _mix_out(x, oa, ga, zc, wba, wout, g2, wg, wu, wd, gf, rows, final):
    n, d = x.shape
    row_spec = pl.BlockSpec((rows, d), lambda i: (i, 0))
    return pl.pallas_call(
        functools.partial(_mix_out_kernel, final=final),
        grid=(n // rows,),
        in_specs=[row_spec] * 4 + [_const_spec(wba.shape), _const_spec(wout.shape), _const_spec((1, d)),
                                   _const_spec(wg.shape), _const_spec(wu.shape), _const_spec(wd.shape),
                                   _const_spec((1, d))],
        out_specs=row_spec,
        out_shape=jax.ShapeDtypeStruct((n, d), F32),
        compiler_params=pltpu.CompilerParams(dimension_semantics=("arbitrary",),
                                             vmem_limit_bytes=V7X_VMEM_LIMIT_BYTES),
        name="mix_out",
    )(x, oa, ga, zc, wba, wout, g2, wg, wu, wd, gf)


def kernel(x_prompt, x_sample, cache_k, cache_v, state_conv, page_table, rel_bias, norm_ffn1, w_ffn1_gate,
           w_ffn1_up, w_ffn1_down, norm_mix, w_in, w_conv, w_branch_attn, w_branch_conv, w_out, norm_ffn2,
           w_ffn2_gate, w_ffn2_up, w_ffn2_down, norm_final):
    batch, seq, d = x_prompt.shape
    n_seq, t_new, _ = x_sample.shape
    depth = w_in.shape[0]
    n_pages = page_table.shape[1]
    page_size = cache_k.shape[2]
    assert t_new == 1 and d == N_HEADS * HEAD_DIM and w_in.shape[2] == 8 * d
    assert seq % PROMPT_ROWS == 0 and PROMPT_ROWS % BLOCK_SIZE == 0 and N_HEADS % ATTN_HEADS == 0
    assert (n_pages * page_size) % BLOCK_SIZE == 0 and n_pages % SCAN_PAGES == 0 and BLOCK_SIZE % page_size == 0
    assert MAX_DISTANCE <= BLOCK_SIZE

    xp = x_prompt.reshape(batch * seq, d)
    xs = x_sample.reshape(n_seq, d)
    pt_flat = page_table.reshape(-1)
    cache_k_t = jnp.transpose(cache_k, (0, 1, 3, 4, 2))
    cache_v_t = jnp.transpose(cache_v, (0, 1, 3, 4, 2))
    bias_tiles = _bias_tiles(rel_bias)
    bias_last = _bias_last_block(rel_bias, page_size)
    outs = {name: [] for name in ("kp", "vp", "cp", "ks", "vs", "cs")}
    for layer in range(depth):
        g1, gm, g2 = (g[layer].reshape(1, d) for g in (norm_ffn1, norm_mix, norm_ffn2))
        gf = norm_final.reshape(1, d)
        wg1, wu1, wd1, win, wba, wbc, wout, wg2, wu2, wd2 = (
            w[layer].astype(BF16) for w in (w_ffn1_gate, w_ffn1_up, w_ffn1_down, w_in, w_branch_attn,
                                            w_branch_conv, w_out, w_ffn2_gate, w_ffn2_up, w_ffn2_down))
        wconv = w_conv[layer]
        final = layer == depth - 1

        xp = _ffn(xp, g1, wg1, wu1, wd1, PROMPT_ROWS)
        q, k, v, ga, zc, conv_state = _mix_in_prompt(xp, gm, win, wconv, wbc, batch, seq, PROMPT_ROWS)
        oa = _attn_prompt(rel_bias, q.reshape(batch, seq, d), k.reshape(batch, seq, d),
                          v.reshape(batch, seq, d), bias_tiles).reshape(batch * seq, d)
        xp = _mix_out(xp, oa, ga, zc, wba, wout, g2, wg2, wu2, wd2, gf, PROMPT_ROWS, final)
        outs["kp"].append(k.reshape(batch, seq, N_HEADS, HEAD_DIM))
        outs["vp"].append(v.reshape(batch, seq, N_HEADS, HEAD_DIM))
        outs["cp"].append(conv_state)

        xs = _ffn(xs, g1, wg1, wu1, wd1, n_seq)
        prev2, prev1 = state_conv[layer, :, 0, :], state_conv[layer, :, 1, :]
        q, k, v, ga, zc, cu = _mix_in_sample(xs, gm, win, wconv, wbc, prev2, prev1)
        q3, k3, v3 = (a.reshape(n_seq, N_HEADS, HEAD_DIM) for a in (q, k, v))
        logits, sel = _scan_keys(pt_flat, jnp.transpose(q3, (0, 2, 1)), cache_k_t, layer, n_pages)
        oa_t = _attn_sample(pt_flat, sel.reshape(-1), rel_bias, logits, q3, k3, jnp.transpose(v3, (0, 2, 1)),
                            bias_last, cache_v_t, layer, n_pages)
        oa = jnp.transpose(oa_t, (0, 2, 1)).reshape(n_seq, d)
        xs = _mix_out(xs, oa, ga, zc, wba, wout, g2, wg2, wu2, wd2, gf, n_seq, final)
        outs["ks"].append(k3.reshape(n_seq, 1, N_HEADS, HEAD_DIM))
        outs["vs"].append(v3.reshape(n_seq, 1, N_HEADS, HEAD_DIM))
        outs["cs"].append(jnp.stack([prev1, cu], axis=1))

    return (xp.reshape(batch, seq, d), xs.reshape(n_seq, 1, d), jnp.stack(outs["kp"]), jnp.stack(outs["vp"]),
            jnp.stack(outs["cp"]), jnp.stack(outs["ks"]), jnp.stack(outs["vs"]), jnp.stack(outs["cs"]))
```

```python
import functools
import math

import jax
import jax.numpy as jnp
from jax import lax
from jax.experimental import pallas as pl
from jax.experimental.pallas import tpu as pltpu

N_HEADS = 16
HEAD_DIM = 64
BLOCK_SIZE = 256
TOP_K = 3
N_BUCKETS = 32
MAX_DISTANCE = 128
CONV_WIDTH = 3
RMS_EPS = 1e-6
SCALE = HEAD_DIM ** -0.5
LOG2_E = math.log2(math.e)

V7X_LANES = 128
V7X_SUBLANES = 8
V7X_VMEM_LIMIT_BYTES = 56 * 1024 * 1024

HEADS_PER_GROUP = V7X_LANES // HEAD_DIM
FF_CHUNK = 256
PROMPT_ROWS = 512
ATTN_HEADS = 4
SCAN_PAGES = 16

F32 = jnp.float32
BF16 = jnp.bfloat16
NEG_INF = float("-inf")


def _const_spec(shape):
    nd = len(shape)
    return pl.BlockSpec(shape, lambda *_: (0,) * nd, pipeline_mode=pl.Buffered(1))


def _rms(x, g):
    ms = jnp.mean(x * x, axis=-1, keepdims=True)
    return (x * lax.rsqrt(ms + RMS_EPS)) * g


def _dot(a, b):
    return jnp.dot(a, b, preferred_element_type=F32)


def _dot_nt(a, b):
    return lax.dot_general(a, b, (((1,), (1,)), ((), ())), preferred_element_type=F32)


def _swiglu(h, wg_ref, wu_ref, wd_ref):
    d_ff = wg_ref.shape[1]
    acc = None
    for c in range(d_ff // FF_CHUNK):
        sl = slice(c * FF_CHUNK, (c + 1) * FF_CHUNK)
        gate = _dot(h, wg_ref[:, sl])
        up = _dot(h, wu_ref[:, sl])
        act = (gate * jax.nn.sigmoid(gate) * up).astype(BF16)
        part = _dot(act, wd_ref[sl, :])
        acc = part if acc is None else acc + part
    return acc


def _ffn_kernel(x_ref, g_ref, wg_ref, wu_ref, wd_ref, o_ref):
    x = x_ref[...]
    h = _rms(x, g_ref[...]).astype(BF16)
    o_ref[...] = x + 0.5 * _swiglu(h, wg_ref, wu_ref, wd_ref)


def _ffn(x, g, wg, wu, wd, rows):
    n, d = x.shape
    d_ff = wg.shape[1]
    row_spec = pl.BlockSpec((rows, d), lambda i: (i, 0))
    return pl.pallas_call(
        _ffn_kernel,
        grid=(n // rows,),
        in_specs=[row_spec, _const_spec((1, d)), _const_spec((d, d_ff)), _const_spec((d, d_ff)),
                  _const_spec((d_ff, d))],
        out_specs=row_spec,
        out_shape=jax.ShapeDtypeStruct((n, d), F32),
        compiler_params=pltpu.CompilerParams(dimension_semantics=("arbitrary",),
                                             vmem_limit_bytes=V7X_VMEM_LIMIT_BYTES),
        name="ffn",
    )(x, g, wg, wu, wd)


def _proj(h, win_ref, n):
    d = h.shape[1]
    return _dot(h, win_ref[:, n * d:(n + 1) * d])


def _mix_in_tail(h, win_ref, wbc_ref, gate_b, conv, ga_ref, zc_ref):
    y_conv = _dot((gate_b * conv).astype(BF16), wbc_ref[...])
    ga_ref[...] = jax.nn.sigmoid(_proj(h, win_ref, 6))
    zc_ref[...] = jax.nn.sigmoid(_proj(h, win_ref, 7)) * y_conv


def _mix_in_prompt_kernel(x_ref, g_ref, win_ref, wconv_ref, wbc_ref,
                          q_ref, kt_ref, vt_ref, kb_ref, ksum_ref, ga_ref, zc_ref, cs_ref, cbuf, *, tiles_per_seq):
    i = pl.program_id(0)
    rows, d = x_ref.shape
    h = _rms(x_ref[...], g_ref[...]).astype(BF16)
    q_ref[...] = _proj(h, win_ref, 0)
    k = _proj(h, win_ref, 1)
    kt_ref[0] = k.T
    kb_ref[...] = k.astype(BF16)
    blocks = rows // BLOCK_SIZE
    sums = [jnp.sum(k[j * BLOCK_SIZE:(j + 1) * BLOCK_SIZE], axis=0, keepdims=True) for j in range(blocks)]
    ksum_ref[0, 0] = jnp.concatenate(sums, axis=0)
    vt_ref[0] = _proj(h, win_ref, 2).T
    gate_b = _proj(h, win_ref, 3)
    cu = _proj(h, win_ref, 4) * _proj(h, win_ref, 5)

    @pl.when(i % tiles_per_seq == 0)
    def _():
        cbuf[0:V7X_SUBLANES, :] = jnp.zeros((V7X_SUBLANES, d), F32)

    cbuf[V7X_SUBLANES:V7X_SUBLANES + rows, :] = cu
    w = wconv_ref[...]
    conv = (w[0:1] * cbuf[V7X_SUBLANES - 2:V7X_SUBLANES - 2 + rows, :]
            + w[1:2] * cbuf[V7X_SUBLANES - 1:V7X_SUBLANES - 1 + rows, :]
            + w[2:3] * cu)
    last = cbuf[rows + V7X_SUBLANES - 2:rows + V7X_SUBLANES, :]
    cs_ref[0] = last
    cbuf[V7X_SUBLANES - 2:V7X_SUBLANES, :] = last
    _mix_in_tail(h, win_ref, wbc_ref, gate_b, conv, ga_ref, zc_ref)


def _mix_in_sample_kernel(x_ref, g_ref, win_ref, wconv_ref, wbc_ref, p2_ref, p1_ref,
                          q_ref, k_ref, v_ref, ga_ref, zc_ref, cu_ref):
    h = _rms(x_ref[...], g_ref[...]).astype(BF16)
    q_ref[...] = _proj(h, win_ref, 0)
    k_ref[...] = _proj(h, win_ref, 1)
    v_ref[...] = _proj(h, win_ref, 2)
    gate_b = _proj(h, win_ref, 3)
    cu = _proj(h, win_ref, 4) * _proj(h, win_ref, 5)
    cu_ref[...] = cu
    w = wconv_ref[...]
    conv = w[0:1] * p2_ref[...] + w[1:2] * p1_ref[...] + w[2:3] * cu
    _mix_in_tail(h, win_ref, wbc_ref, gate_b, conv, ga_ref, zc_ref)


def _mix_in_prompt(x, g, win, wconv, wbc, batch, seq, rows):
    n, d = x.shape
    tiles_per_seq = seq // rows
    row_spec = pl.BlockSpec((rows, d), lambda i: (i, 0))
    row_shape = jax.ShapeDtypeStruct((n, d), F32)
    t_spec = pl.BlockSpec((1, d, rows), lambda i: (i // tiles_per_seq, 0, i % tiles_per_seq))
    t_shape = jax.ShapeDtypeStruct((batch, d, seq), F32)
    per_seq = lambda r: pl.BlockSpec((1, r, d), lambda i: (i // tiles_per_seq, 0, 0))
    blocks = rows // BLOCK_SIZE
    return pl.pallas_call(
        functools.partial(_mix_in_prompt_kernel, tiles_per_seq=tiles_per_seq),
        grid=(n // rows,),
        in_specs=[row_spec, _const_spec((1, d)), _const_spec(win.shape), _const_spec(wconv.shape),
                  _const_spec(wbc.shape)],
        out_specs=[row_spec, t_spec, t_spec, row_spec,
                   pl.BlockSpec((1, 1, blocks, d), lambda i: (i // tiles_per_seq, i % tiles_per_seq, 0, 0)),
                   row_spec, row_spec, per_seq(CONV_WIDTH - 1)],
        out_shape=[row_shape, t_shape, t_shape, jax.ShapeDtypeStruct((n, d), BF16),
                   jax.ShapeDtypeStruct((batch, tiles_per_seq, blocks, d), F32), row_shape, row_shape,
                   jax.ShapeDtypeStruct((batch, CONV_WIDTH - 1, d), F32)],
        scratch_shapes=[pltpu.VMEM((rows + V7X_SUBLANES, d), F32)],
        compiler_params=pltpu.CompilerParams(dimension_semantics=("arbitrary",),
                                             vmem_limit_bytes=V7X_VMEM_LIMIT_BYTES),
        name="mix_in_prompt",
    )(x, g, win, wconv, wbc)


def _mix_in_sample(x, g, win, wconv, wbc, prev2, prev1):
    n, d = x.shape
    full = pl.BlockSpec((n, d), lambda i: (0, 0))
    row_shape = jax.ShapeDtypeStruct((n, d), F32)
    return pl.pallas_call(
        _mix_in_sample_kernel,
        grid=(1,),
        in_specs=[full, _const_spec((1, d)), _const_spec(win.shape), _const_spec(wconv.shape),
                  _const_spec(wbc.shape), full, full],
        out_specs=[full] * 6,
        out_shape=[row_shape] * 6,
        compiler_params=pltpu.CompilerParams(dimension_semantics=("arbitrary",),
                                             vmem_limit_bytes=V7X_VMEM_LIMIT_BYTES),
        name="mix_in_sample",
    )(x, g, win, wconv, wbc, prev2, prev1)


def _t5_bucket(dist):
    n = jnp.maximum(dist, 0)
    max_exact = N_BUCKETS // 2
    nf = jnp.maximum(n, 1).astype(F32)
    large = max_exact + (jnp.log(nf / max_exact) / math.log(MAX_DISTANCE / max_exact)
                         * (N_BUCKETS - max_exact)).astype(jnp.int32)
    return jnp.where(n < max_exact, n, jnp.minimum(large, N_BUCKETS - 1))


def _bias_lookup(dist, table_row):
    bucket = _t5_bucket(dist)
    bias = jnp.zeros(dist.shape, F32)
    for b in range(N_BUCKETS):
        bias = jnp.where(bucket == b, table_row(b), bias)
    return bias


def _bias_tiles_kernel(rb_ref, o_ref):
    h = pl.program_id(0)
    kk = lax.broadcasted_iota(jnp.int32, (BLOCK_SIZE, BLOCK_SIZE), 0)
    qq = lax.broadcasted_iota(jnp.int32, (BLOCK_SIZE, BLOCK_SIZE), 1)
    own = _bias_lookup(qq - kk, lambda b: rb_ref[b, h]) * LOG2_E
    o_ref[0, 0] = jnp.where(qq >= kk, own, NEG_INF)
    o_ref[0, 1] = _bias_lookup(qq - kk + BLOCK_SIZE, lambda b: rb_ref[b, h]) * LOG2_E


def _bias_tiles(rel_bias):
    return pl.pallas_call(
        _bias_tiles_kernel,
        grid=(N_HEADS,),
        in_specs=[pl.BlockSpec(memory_space=pltpu.SMEM)],
        out_specs=pl.BlockSpec((1, 2, BLOCK_SIZE, BLOCK_SIZE), lambda h: (h, 0, 0, 0)),
        out_shape=jax.ShapeDtypeStruct((N_HEADS, 2, BLOCK_SIZE, BLOCK_SIZE), F32),
        name="bias_tiles",
    )(rel_bias)


def _bias_last_block_kernel(rb_ref, o_ref):
    h = pl.program_id(0)
    _, n_p, page_size = o_ref.shape
    p = lax.broadcasted_iota(jnp.int32, (n_p, page_size), 0)
    r = lax.broadcasted_iota(jnp.int32, (n_p, page_size), 1)
    o_ref[0] = _bias_lookup(BLOCK_SIZE - (p * page_size + r), lambda b: rb_ref[b, h])


def _bias_last_block(rel_bias, page_size):
    shape = (N_HEADS, BLOCK_SIZE // page_size, page_size)
    return pl.pallas_call(
        _bias_last_block_kernel,
        grid=(N_HEADS,),
        in_specs=[pl.BlockSpec(memory_space=pltpu.SMEM)],
        out_specs=pl.BlockSpec((1,) + shape[1:], lambda h: (h, 0, 0)),
        out_shape=jax.ShapeDtypeStruct(shape, F32),
        name="bias_last_block",
    )(rel_bias)


def _attn_prompt_kernel(rb_ref, q_ref, k_ref, vt_ref, ksum_ref, bt_ref, o_ref, kb_sc, vt_sc, sel_sc):
    hq = pl.program_id(1)
    qb = pl.program_id(2)
    n_blk = k_ref.shape[1] // BLOCK_SIZE
    n_grp = q_ref.shape[2] // V7X_LANES
    n_heads = n_grp * HEADS_PER_GROUP
    bs = BLOCK_SIZE

    @pl.when(qb == 0)
    def _():
        kb_sc[0:bs, :] = jnp.zeros((bs, kb_sc.shape[1]), BF16)
        kb_sc[bs:, :] = k_ref[0]
        sel_sc[:, n_blk:, :] = jnp.zeros((n_heads, sel_sc.shape[1] - n_blk, bs), F32)
        sub = lax.broadcasted_iota(jnp.int32, (V7X_SUBLANES * 2, bs), 0)
        ones_rows = jnp.where(sub == 0, 1.0, 0.0).astype(BF16)
        for g in range(n_grp):
            vt_sc[g, 0] = jnp.zeros(vt_sc.shape[2:], BF16)
            for j in range(n_blk):
                vt_sc[g, j + 1, 0:V7X_LANES, :] = vt_ref[0, g * V7X_LANES:(g + 1) * V7X_LANES,
                                                         j * bs:(j + 1) * bs].astype(BF16)
                vt_sc[g, j + 1, V7X_LANES:, :] = ones_rows

    q = q_ref[0] * (SCALE * LOG2_E)
    lane = lax.broadcasted_iota(jnp.int32, (bs, V7X_LANES), 1)
    blk_row = lax.broadcasted_iota(jnp.int32, (n_blk, bs), 0)
    qms = []
    for c in range(n_heads):
        g, i = divmod(c, HEADS_PER_GROUP)
        lanes = slice(g * V7X_LANES, (g + 1) * V7X_LANES)
        qm = jnp.where(lane // HEAD_DIM == i, q[:, lanes], 0.0).astype(BF16)
        qms.append(qm)
        st = _dot_nt((ksum_ref[0, :, lanes] / BLOCK_SIZE).astype(BF16), qm)
        rank = jnp.zeros(st.shape, jnp.int32)
        for j in range(n_blk):
            sj = st[j:j + 1, :]
            beats = (sj > st) | ((sj == st) & (j < blk_row))
            rank = rank + jnp.where(beats & (j < qb), 1, 0)
        sel_sc[c, 0:n_blk, :] = jnp.where((rank < TOP_K) & (blk_row < qb), 1.0, 0.0)

    def col_max(x):
        return jnp.max(x, axis=0, keepdims=True)

    def window(c, first_padded_block):
        g = c // HEADS_PER_GROUP
        rows = pl.ds(pl.multiple_of(first_padded_block * bs, bs), 2 * bs)
        s = _dot_nt(kb_sc[rows, g * V7X_LANES:(g + 1) * V7X_LANES], qms[c])
        return s[:bs], s[bs:]

    prev_row = jnp.where(qb == 0, n_blk, qb - 1)
    scores = []
    for c in range(n_heads):
        s_prev, s_own = window(c, qb)
        s_prev = s_prev + bt_ref[c, 1] + jnp.where(sel_sc[c, pl.ds(prev_row, 1), :] > 0.5, 0.0, NEG_INF)
        scores.append((s_prev, s_own + bt_ref[c, 0]))
    states = []
    for c in range(n_heads):
        g = c // HEADS_PER_GROUP
        s_prev, s_own = scores[c]
        m = jnp.maximum(col_max(s_prev), col_max(s_own))
        p_prev, p_own = jnp.exp2(s_prev - m), jnp.exp2(s_own - m)
        acc = _dot(vt_sc[g, qb], p_prev.astype(BF16)) + _dot(vt_sc[g, qb + 1], p_own.astype(BF16))
        states.append((m, acc))

    n_far = jnp.maximum(qb - 1, 0)

    def far_body(t, states):
        j0 = 2 * t
        scores = []
        for c in range(n_heads):
            far_bias = rb_ref[N_BUCKETS - 1, hq * n_heads + c] * LOG2_E
            s0, s1 = window(c, j0 + 1)
            s0 = s0 + jnp.where(sel_sc[c, pl.ds(j0, 1), :] > 0.5, far_bias, NEG_INF)
            s1 = s1 + jnp.where((sel_sc[c, pl.ds(j0 + 1, 1), :] > 0.5) & (j0 + 1 < n_far), far_bias, NEG_INF)
            scores.append((s0, s1))
        new = []
        for c in range(n_heads):
            g = c // HEADS_PER_GROUP
            s0, s1 = scores[c]
            m, acc = states[c]
            m_new = jnp.maximum(m, jnp.maximum(col_max(s0), col_max(s1)))
            alpha = jnp.exp2(m - m_new)
            p0, p1 = jnp.exp2(s0 - m_new), jnp.exp2(s1 - m_new)
            acc_new = (alpha * acc + _dot(vt_sc[g, j0 + 1], p0.astype(BF16))
                       + _dot(vt_sc[g, j0 + 2], p1.astype(BF16)))
            new.append((m_new, acc_new))
        return tuple(new)

    states = lax.fori_loop(0, (n_far + 1) // 2, far_body, tuple(states))
    dim_row = lax.broadcasted_iota(jnp.int32, (V7X_LANES, bs), 0)
    for g in range(n_grp):
        out_t = jnp.zeros((V7X_LANES, bs), F32)
        for i in range(HEADS_PER_GROUP):
            m, acc = states[g * HEADS_PER_GROUP + i]
            out_t = jnp.where(dim_row // HEAD_DIM == i, acc[:V7X_LANES] * (1.0 / acc[V7X_LANES:V7X_LANES + 1]), out_t)
        o_ref[0, :, g * V7X_LANES:(g + 1) * V7X_LANES] = out_t.T


def _attn_prompt(rel_bias, q, k_bf16, v_t, k_sums, bias_tiles):
    batch, seq, d = q.shape
    n_blk = seq // BLOCK_SIZE
    width = ATTN_HEADS * HEAD_DIM
    n_grp = width // V7X_LANES
    q_spec = pl.BlockSpec((1, BLOCK_SIZE, width), lambda b, g, i: (b, i, g))
    return pl.pallas_call(
        _attn_prompt_kernel,
        grid=(batch, d // width, n_blk),
        in_specs=[pl.BlockSpec(memory_space=pltpu.SMEM), q_spec,
                  pl.BlockSpec((1, seq, width), lambda b, g, i: (b, 0, g)),
                  pl.BlockSpec((1, width, seq), lambda b, g, i: (b, g, 0)),
                  pl.BlockSpec((1, n_blk, width), lambda b, g, i: (b, 0, g)),
                  pl.BlockSpec((ATTN_HEADS, 2, BLOCK_SIZE, BLOCK_SIZE), lambda b, g, i: (g, 0, 0, 0))],
        out_specs=q_spec,
        out_shape=jax.ShapeDtypeStruct((batch, seq, d), F32),
        scratch_shapes=[pltpu.VMEM((seq + BLOCK_SIZE, width), BF16),
                        pltpu.VMEM((n_grp, n_blk + 1, V7X_LANES + 2 * V7X_SUBLANES, BLOCK_SIZE), BF16),
                        pltpu.VMEM((ATTN_HEADS, 2 * V7X_SUBLANES, BLOCK_SIZE), F32)],
        compiler_params=pltpu.CompilerParams(dimension_semantics=("arbitrary", "arbitrary", "arbitrary"),
                                             vmem_limit_bytes=V7X_VMEM_LIMIT_BYTES),
        name="attn_prompt",
    )(rel_bias, q, k_bf16, v_t, k_sums, bias_tiles)


def _scan_keys_kernel(pt_ref, qt_ref, ck_ref, lg_ref, sel_ref, pages, qb_sc, pg_sc, sem, *, layer, n_groups):
    b = pl.program_id(0)
    g = pl.program_id(1)
    step = b * n_groups + g
    n_steps = pl.num_programs(0) * n_groups
    page_size = ck_ref.shape[-1]
    pages_per_block = BLOCK_SIZE // page_size
    n_blocks = n_groups * SCAN_PAGES // pages_per_block

    def copies(step, slot):
        return [pltpu.make_async_copy(ck_ref.at[layer, pt_ref[step * SCAN_PAGES + i]], pages.at[slot, i],
                                      sem.at[slot]) for i in range(SCAN_PAGES)]

    slot = step % 2

    @pl.when(step == 0)
    def _():
        for c in copies(0, 0):
            c.start()

    @pl.when(step + 1 < n_steps)
    def _():
        for c in copies(step + 1, 1 - slot):
            c.start()

    @pl.when(g == 0)
    def _():
        qt = qt_ref[0]
        for h in range(N_HEADS):
            qb_sc[h] = jnp.broadcast_to(qt[:, h:h + 1], (HEAD_DIM, page_size))

    for c in copies(step, slot):
        c.wait()

    def page_body(i, carry):
        for h in range(N_HEADS):
            lg_ref[0, h, pl.ds(i, 1), :] = jnp.sum(pages[slot, i, h] * qb_sc[h], axis=0, keepdims=True)
        return carry

    lax.fori_loop(0, SCAN_PAGES, page_body, 0)

    lane = lax.broadcasted_iota(jnp.int32, (SCAN_PAGES, V7X_LANES), 1)
    tile = jnp.zeros((SCAN_PAGES, V7X_LANES), F32)
    for h in range(N_HEADS):
        tile = jnp.where(lane == h, jnp.sum(lg_ref[0, h], axis=1, keepdims=True), tile)
    pg_sc[pl.ds(pl.multiple_of(g * SCAN_PAGES, SCAN_PAGES), SCAN_PAGES), :] = tile

    @pl.when(g == n_groups - 1)
    def _():
        cur = pg_sc[pl.ds(0, n_blocks, stride=pages_per_block), :]
        for p in range(1, pages_per_block):
            cur = cur + pg_sc[pl.ds(p, n_blocks, stride=pages_per_block), :]
        cur = cur / BLOCK_SIZE
        row = lax.broadcasted_iota(jnp.int32, cur.shape, 0)
        out_row = lax.broadcasted_iota(jnp.int32, sel_ref.shape[1:], 0)
        out = jnp.zeros(sel_ref.shape[1:], jnp.int32)
        for t in range(TOP_K):
            best = jnp.max(cur, axis=0, keepdims=True)
            idx = jnp.min(jnp.where(cur == best, row, n_blocks), axis=0, keepdims=True)
            out = jnp.where(out_row == t, idx, out)
            cur = jnp.where(row == idx, NEG_INF, cur)
        sel_ref[0] = out


def _scan_keys(page_table_flat, q_t, cache_k_t, layer, n_pages):
    n_seq = q_t.shape[0]
    page_size = cache_k_t.shape[-1]
    n_groups = n_pages // SCAN_PAGES
    logits, sel = pl.pallas_call(
        functools.partial(_scan_keys_kernel, layer=layer, n_groups=n_groups),
        grid_spec=pltpu.PrefetchScalarGridSpec(
            num_scalar_prefetch=1,
            grid=(n_seq, n_groups),
            in_specs=[pl.BlockSpec((1, HEAD_DIM, N_HEADS), lambda b, g, pt: (b, 0, 0)),
                      pl.BlockSpec(memory_space=pl.ANY)],
            out_specs=[pl.BlockSpec((1, N_HEADS, SCAN_PAGES, page_size), lambda b, g, pt: (b, 0, g, 0)),
                       pl.BlockSpec((1, V7X_SUBLANES, V7X_LANES), lambda b, g, pt: (b, 0, 0))],
            scratch_shapes=[pltpu.VMEM((2, SCAN_PAGES, N_HEADS, HEAD_DIM, page_size), F32),
                            pltpu.VMEM((N_HEADS, HEAD_DIM, page_size), F32),
                            pltpu.VMEM((n_pages, V7X_LANES), F32),
                            pltpu.SemaphoreType.DMA((2,))]),
        out_shape=[jax.ShapeDtypeStruct((n_seq, N_HEADS, n_pages, page_size), F32),
                   jax.ShapeDtypeStruct((n_seq, V7X_SUBLANES, V7X_LANES), jnp.int32)],
        compiler_params=pltpu.CompilerParams(dimension_semantics=("arbitrary", "arbitrary"),
                                             vmem_limit_bytes=V7X_VMEM_LIMIT_BYTES),
        name="scan_keys",
    )(page_table_flat, q_t, cache_k_t)
    return logits, jnp.transpose(sel[:, :TOP_K, :N_HEADS], (0, 2, 1))


def _attn_sample_kernel(pt_ref, sel_ref, rb_ref, lg_ref, q_ref, kn_ref, vnt_ref, bl_ref, cv_ref, ot_ref,
                        vbuf, sem, *, layer, n_pages):
    b = pl.program_id(0)
    n_seq = pl.num_programs(0)
    page_size = cv_ref.shape[-1]
    pages_per_block = BLOCK_SIZE // page_size
    last_block = n_pages // pages_per_block - 1

    def selected(seq, h, s):
        return sel_ref[(seq * N_HEADS + h) * TOP_K + s]

    def copies(seq, slot):
        out = []
        for h in range(N_HEADS):
            for s in range(TOP_K):
                blk = selected(seq, h, s)
                for p in range(pages_per_block):
                    phys = pt_ref[seq * n_pages + blk * pages_per_block + p]
                    out.append(pltpu.make_async_copy(cv_ref.at[layer, phys, h],
                                                     vbuf.at[slot, h, s * pages_per_block + p], sem.at[slot]))
        return out

    slot = b % 2

    @pl.when(b == 0)
    def _():
        for c in copies(0, 0):
            c.start()

    @pl.when(b + 1 < n_seq)
    def _():
        for c in copies(b + 1, 1 - slot):
            c.start()

    for c in copies(b, slot):
        c.wait()

    def all_max(x):
        return jnp.max(jnp.max(x, axis=1, keepdims=True), axis=0, keepdims=True)

    def all_sum(x):
        return jnp.sum(jnp.sum(x, axis=1, keepdims=True), axis=0, keepdims=True)

    new_dots = jnp.sum(q_ref[0] * kn_ref[0], axis=1, keepdims=True) * SCALE
    for h in range(N_HEADS):
        logit_new = new_dots[h:h + 1, :] + rb_ref[0, h]
        pieces = []
        for s in range(TOP_K):
            blk = selected(b, h, s)
            raw = lg_ref[0, h, pl.ds(pl.multiple_of(blk * pages_per_block, pages_per_block), pages_per_block), :]
            bias = jnp.where(blk == last_block, bl_ref[h], rb_ref[N_BUCKETS - 1, h])
            pieces.append(raw * SCALE + bias)
        m = logit_new
        for x in pieces:
            m = jnp.maximum(m, all_max(x))
        p_new = jnp.exp(logit_new - m)
        denom = p_new
        acc = None
        for s, x in enumerate(pieces):
            p = jnp.exp(x - m)
            denom = denom + all_sum(p)
            for pg in range(pages_per_block):
                term = p[pg:pg + 1, :] * vbuf[slot, h, s * pages_per_block + pg]
                acc = term if acc is None else acc + term
        num = jnp.sum(acc, axis=1, keepdims=True) + p_new * vnt_ref[0, :, h:h + 1]
        ot_ref[0, :, h:h + 1] = num / denom


def _attn_sample(page_table_flat, sel_flat, rel_bias, logits, q, k_new, v_new_t, bias_last, cache_v_t, layer,
                 n_pages):
    n_seq = q.shape[0]
    page_size = cache_v_t.shape[-1]
    pages_per_block = BLOCK_SIZE // page_size
    seq_spec = pl.BlockSpec((1, N_HEADS, HEAD_DIM), lambda b, pt, sel: (b, 0, 0))
    seq_t_spec = pl.BlockSpec((1, HEAD_DIM, N_HEADS), lambda b, pt, sel: (b, 0, 0))
    return pl.pallas_call(
        functools.partial(_attn_sample_kernel, layer=layer, n_pages=n_pages),
        grid_spec=pltpu.PrefetchScalarGridSpec(
            num_scalar_prefetch=2,
            grid=(n_seq,),
            in_specs=[pl.BlockSpec(memory_space=pltpu.SMEM),
                      pl.BlockSpec((1, N_HEADS, n_pages, page_size), lambda b, pt, sel: (b, 0, 0, 0)),
                      seq_spec, seq_spec, seq_t_spec,
                      pl.BlockSpec(bias_last.shape, lambda b, pt, sel: (0, 0, 0)),
                      pl.BlockSpec(memory_space=pl.ANY)],
            out_specs=seq_t_spec,
            scratch_shapes=[pltpu.VMEM((2, N_HEADS, TOP_K * pages_per_block, HEAD_DIM, page_size), F32),
                            pltpu.SemaphoreType.DMA((2,))]),
        out_shape=jax.ShapeDtypeStruct((n_seq, HEAD_DIM, N_HEADS), F32),
        compiler_params=pltpu.CompilerParams(dimension_semantics=("arbitrary",),
                                             vmem_limit_bytes=V7X_VMEM_LIMIT_BYTES),
        name="attn_sample",
    )(page_table_flat, sel_flat, rel_bias, logits, q, k_new, v_new_t, bias_last, cache_v_t)


def _mix_out_kernel(x_ref, oa_ref, ga_ref, zc_ref, wba_ref, wout_ref, g2_ref, wg_ref, wu_ref, wd_ref, gf_ref,
                    y_ref, *, final):
    y_attn = _dot(oa_ref[...].astype(BF16), wba_ref[...])
    merged = ga_ref[...] * y_attn + zc_ref[...]
    x = x_ref[...] + _dot(merged.astype(BF16), wout_ref[...])
    h = _rms(x, g2_ref[...]).astype(BF16)
    x = x + 0.5 * _swiglu(h, wg_ref, wu_ref, wd_ref)
    y_ref[...] = _rms(x, gf_ref[...]) if final else x


def _mix_out(x, oa, ga, zc, wba, wout, g2, wg, wu, wd, gf, rows, final):
    n, d = x.shape
    row_spec = pl.BlockSpec((rows, d), lambda i: (i, 0))
    return pl.pallas_call(
        functools.partial(_mix_out_kernel, final=final),
        grid=(n // rows,),
        in_specs=[row_spec] * 4 + [_const_spec(wba.shape), _const_spec(wout.shape), _const_spec((1, d)),
                                   _const_spec(wg.shape), _const_spec(wu.shape), _const_spec(wd.shape),
                                   _const_spec((1, d))],
        out_specs=row_spec,
        out_shape=jax.ShapeDtypeStruct((n, d), F32),
        compiler_params=pltpu.CompilerParams(dimension_semantics=("arbitrary",),
                                             vmem_limit_bytes=V7X_VMEM_LIMIT_BYTES),
        name="mix_out",
    )(x, oa, ga, zc, wba, wout, g2, wg, wu, wd, gf)


def kernel(x_prompt, x_sample, cache_k, cache_v, state_conv, page_table, rel_bias, norm_ffn1, w_ffn1_gate,
           w_ffn1_up, w_ffn1_down, norm_mix, w_in, w_conv, w_branch_attn, w_branch_conv, w_out, norm_ffn2,
           w_ffn2_gate, w_ffn2_up, w_ffn2_down, norm_final):
    batch, seq, d = x_prompt.shape
    n_seq, t_new, _ = x_sample.shape
    depth = w_in.shape[0]
    n_pages = page_table.shape[1]
    page_size = cache_k.shape[2]
    assert t_new == 1 and d == N_HEADS * HEAD_DIM and w_in.shape[2] == 8 * d
    assert seq % PROMPT_ROWS == 0 and PROMPT_ROWS % BLOCK_SIZE == 0 and N_HEADS % ATTN_HEADS == 0
    assert (n_pages * page_size) % BLOCK_SIZE == 0 and n_pages % SCAN_PAGES == 0 and BLOCK_SIZE % page_size == 0
    assert MAX_DISTANCE <= BLOCK_SIZE

    xp = x_prompt.reshape(batch * seq, d)
    xs = x_sample.reshape(n_seq, d)
    pt_flat = page_table.reshape(-1)
    cache_k_t = jnp.transpose(cache_k, (0, 1, 3, 4, 2))
    cache_v_t = jnp.transpose(cache_v, (0, 1, 3, 4, 2))
    bias_tiles = _bias_tiles(rel_bias)
    bias_last = _bias_last_block(rel_bias, page_size)
    outs = {name: [] for name in ("kp", "vp", "cp", "ks", "vs", "cs")}
    for layer in range(depth):
        g1, gm, g2 = (g[layer].reshape(1, d) for g in (norm_ffn1, norm_mix, norm_ffn2))
        gf = norm_final.reshape(1, d)
        wg1, wu1, wd1, win, wba, wbc, wout, wg2, wu2, wd2 = (
            w[layer].astype(BF16) for w in (w_ffn1_gate, w_ffn1_up, w_ffn1_down, w_in, w_branch_attn,
                                            w_branch_conv, w_out, w_ffn2_gate, w_ffn2_up, w_ffn2_down))
        wconv = w_conv[layer]
        final = layer == depth - 1

        xp = _ffn(xp, g1, wg1, wu1, wd1, PROMPT_ROWS)
        q, k_t, v_t, k_bf16, k_sums, ga, zc, conv_state = _mix_in_prompt(xp, gm, win, wconv, wbc, batch, seq,
                                                                        PROMPT_ROWS)
        oa = _attn_prompt(rel_bias, q.reshape(batch, seq, d), k_bf16.reshape(batch, seq, d), v_t,
                          k_sums.reshape(batch, seq // BLOCK_SIZE, d), bias_tiles).reshape(batch * seq, d)
        xp = _mix_out(xp, oa, ga, zc, wba, wout, g2, wg2, wu2, wd2, gf, PROMPT_ROWS, final)
        for name, a_t in (("kp", k_t), ("vp", v_t)):
            outs[name].append(jnp.transpose(a_t.reshape(batch, N_HEADS, HEAD_DIM, seq), (0, 3, 1, 2)))
        outs["cp"].append(conv_state)

        xs = _ffn(xs, g1, wg1, wu1, wd1, n_seq)
        prev2, prev1 = state_conv[layer, :, 0, :], state_conv[layer, :, 1, :]
        q, k, v, ga, zc, cu = _mix_in_sample(xs, gm, win, wconv, wbc, prev2, prev1)
        q3, k3, v3 = (a.reshape(n_seq, N_HEADS, HEAD_DIM) for a in (q, k, v))
        logits, sel = _scan_keys(pt_flat, jnp.transpose(q3, (0, 2, 1)), cache_k_t, layer, n_pages)
        oa_t = _attn_sample(pt_flat, sel.reshape(-1), rel_bias, logits, q3, k3, jnp.transpose(v3, (0, 2, 1)),
                            bias_last, cache_v_t, layer, n_pages)
        oa = jnp.transpose(oa_t, (0, 2, 1)).reshape(n_seq, d)
        xs = _mix_out(xs, oa, ga, zc, wba, wout, g2, wg2, wu2, wd2, gf, n_seq, final)
        outs["ks"].append(k3.reshape(n_seq, 1, N_HEADS, HEAD_DIM))
        outs["vs"].append(v3.reshape(n_seq, 1, N_HEADS, HEAD_DIM))
        outs["cs"].append(jnp.stack([prev1, cu], axis=1))

    return (xp.reshape(batch, seq, d), xs.reshape(n_seq, 1, d), jnp.stack(outs["kp"]), jnp.stack(outs["vp"]),
            jnp.stack(outs["cp"]), jnp.stack(outs["ks"]), jnp.stack(outs["vs"]), jnp.stack(outs["cs"]))
```

```python
import functools
import math

import jax
import jax.numpy as jnp
from jax import lax
from jax.experimental import pallas as pl
from jax.experimental.pallas import tpu as pltpu

N_HEADS = 16
HEAD_DIM = 64
BLOCK_SIZE = 256
TOP_K = 3
N_BUCKETS = 32
MAX_DISTANCE = 128
CONV_WIDTH = 3
RMS_EPS = 1e-6
SCALE = HEAD_DIM ** -0.5
LOG2_E = math.log2(math.e)

V7X_LANES = 128
V7X_SUBLANES = 8
V7X_VMEM_LIMIT_BYTES = 60 * 1024 * 1024

HEADS_PER_GROUP = V7X_LANES // HEAD_DIM
FF_CHUNK = 256
PROMPT_ROWS = 512
MIX_OUT_ROWS = 256
ATTN_HEADS = 4
SCAN_PAGES = 16
SCAN_RING = 16

F32 = jnp.float32
BF16 = jnp.bfloat16
NEG_INF = float("-inf")


def _const_spec(shape):
    nd = len(shape)
    return pl.BlockSpec(shape, lambda *_: (0,) * nd, pipeline_mode=pl.Buffered(1))


def _rms(x, g):
    ms = jnp.mean(x * x, axis=-1, keepdims=True)
    return (x * lax.rsqrt(ms + RMS_EPS)) * g


def _dot(a, b):
    return jnp.dot(a, b, preferred_element_type=F32)


def _dot_nt(a, b):
    return lax.dot_general(a, b, (((1,), (1,)), ((), ())), preferred_element_type=F32)


def _swiglu(h, wg_ref, wu_ref, wd_ref, side_work=None):
    d_ff = wg_ref.shape[1]
    n_chunks = d_ff // FF_CHUNK
    acc = None
    for c in range(n_chunks):
        sl = slice(c * FF_CHUNK, (c + 1) * FF_CHUNK)
        if side_work is not None:
            side_work(c, n_chunks, 0)
        gate = _dot(h, wg_ref[:, sl])
        if side_work is not None:
            side_work(c, n_chunks, 1)
        up = _dot(h, wu_ref[:, sl])
        if side_work is not None:
            side_work(c, n_chunks, 2)
        act = (gate * jax.nn.sigmoid(gate) * up).astype(BF16)
        part = _dot(act, wd_ref[sl, :])
        acc = part if acc is None else acc + part
        if side_work is not None:
            side_work(c, n_chunks, 3)
    return acc


def _topk_blocks(pg_sc, sel_ref, n_blocks, pages_per_block):
    cur = pg_sc[pl.ds(0, n_blocks, stride=pages_per_block), :]
    for p in range(1, pages_per_block):
        cur = cur + pg_sc[pl.ds(p, n_blocks, stride=pages_per_block), :]
    cur = cur / BLOCK_SIZE
    row = lax.broadcasted_iota(jnp.int32, cur.shape, 0)
    out_row = lax.broadcasted_iota(jnp.int32, sel_ref.shape[1:], 0)
    out = jnp.zeros(sel_ref.shape[1:], jnp.int32)
    for t in range(TOP_K):
        best = jnp.max(cur, axis=0, keepdims=True)
        idx = jnp.min(jnp.where(cur == best, row, n_blocks), axis=0, keepdims=True)
        out = jnp.where(out_row == t, idx, out)
        cur = jnp.where(row == idx, NEG_INF, cur)
    sel_ref[0] = out


def _page_sums(lg_ref, n_rows):
    lane = lax.broadcasted_iota(jnp.int32, (n_rows, V7X_LANES), 1)
    tile = jnp.zeros((n_rows, V7X_LANES), F32)
    for h in range(N_HEADS):
        tile = jnp.where(lane == h, jnp.sum(lg_ref[0, h], axis=1, keepdims=True), tile)
    return tile


def _hosted_scan(pt_ref, qt_ref, ck_ref, lg_ref, sel_ref, ring, qb_sc, pg_sc, sem, *, layer):
    step = pl.program_id(0)
    n_steps = pl.num_programs(0)
    n_pages, page_size = lg_ref.shape[2:]
    last_page = n_steps * n_pages - 1

    def copy(pg, ahead=0):
        n = jnp.minimum(step * n_pages + pg + ahead, last_page)
        slot = (pg + ahead) % SCAN_RING
        return pltpu.make_async_copy(ck_ref.at[layer, pt_ref[n]], ring.at[slot], sem.at[slot])

    @pl.when(step == 0)
    def _():
        for pg in range(SCAN_RING):
            copy(pg).start()

    qt = qt_ref[0]
    for h in range(N_HEADS):
        qb_sc[h] = jnp.broadcast_to(qt[:, h:h + 1], (HEAD_DIM, page_size))

    def side_work(c, n_chunks, phase):
        per_chunk = -(-n_pages // n_chunks)
        pages = range(c * per_chunk, min(n_pages, (c + 1) * per_chunk))
        if phase == 0:
            for pg in pages:
                copy(pg).wait()
            return
        share = -(-len(pages) // 3)
        for h in range(N_HEADS):
            q_h = qb_sc[h]
            for pg in pages[(phase - 1) * share:phase * share]:
                lg_ref[0, h, pg:pg + 1, :] = jnp.sum(ring[pg % SCAN_RING, h] * q_h, axis=0, keepdims=True)
        if phase == 3:
            for pg in pages:
                copy(pg, SCAN_RING).start()

    def finish():
        pg_sc[...] = _page_sums(lg_ref, n_pages)
        _topk_blocks(pg_sc, sel_ref, n_pages * page_size // BLOCK_SIZE, BLOCK_SIZE // page_size)

        @pl.when(step == n_steps - 1)
        def _():
            for pg in range(n_pages - SCAN_RING, n_pages):
                copy(pg, SCAN_RING).wait()

    return side_work, finish


def _hosted_scan_specs(n_host, n_pages, page_size):
    in_specs = [pl.BlockSpec((1, HEAD_DIM, N_HEADS), lambda i, *_: (i, 0, 0)), pl.BlockSpec(memory_space=pl.ANY)]
    out_specs = [pl.BlockSpec((1, N_HEADS, n_pages, page_size), lambda i, *_: (i, 0, 0, 0)),
                 pl.BlockSpec((1, V7X_SUBLANES, V7X_LANES), lambda i, *_: (i, 0, 0))]
    out_shape = [jax.ShapeDtypeStruct((n_host, N_HEADS, n_pages, page_size), F32),
                 jax.ShapeDtypeStruct((n_host, V7X_SUBLANES, V7X_LANES), jnp.int32)]
    scratch = [pltpu.VMEM((SCAN_RING, N_HEADS, HEAD_DIM, page_size), F32),
               pltpu.VMEM((N_HEADS, HEAD_DIM, page_size), F32),
               pltpu.VMEM((n_pages, V7X_LANES), F32),
               pltpu.SemaphoreType.DMA((SCAN_RING,))]
    return in_specs, out_specs, out_shape, scratch


def _sel_to_blocks(sel):
    return jnp.transpose(sel[:, :TOP_K, :N_HEADS], (0, 2, 1))


def _ffn_kernel(*refs, scan_layer):
    if scan_layer is None:
        x_ref, g_ref, wg_ref, wu_ref, wd_ref, o_ref = refs
        side_work = finish = None
    else:
        pt_ref, x_ref, g_ref, wg_ref, wu_ref, wd_ref, qt_ref, ck_ref, o_ref, lg_ref, sel_ref = refs[:11]
        side_work, finish = _hosted_scan(pt_ref, qt_ref, ck_ref, lg_ref, sel_ref, *refs[11:], layer=scan_layer)
    x = x_ref[...]
    h = _rms(x, g_ref[...]).astype(BF16)
    o_ref[...] = x + 0.5 * _swiglu(h, wg_ref, wu_ref, wd_ref, side_work)
    if finish is not None:
        finish()


def _ffn(x, g, wg, wu, wd, rows, scan=None):
    n, d = x.shape
    d_ff = wg.shape[1]
    row_spec = pl.BlockSpec((rows, d), lambda i, *_: (i, 0))
    in_specs = [row_spec, _const_spec((1, d)), _const_spec((d, d_ff)), _const_spec((d, d_ff)), _const_spec((d_ff, d))]
    out_specs, out_shape, scratch, prefetch, operands = [row_spec], [jax.ShapeDtypeStruct((n, d), F32)], [], [], []
    if scan is not None:
        pt, q_t, cache_k_t, layer = scan
        assert q_t.shape[0] == n // rows
        extra = _hosted_scan_specs(q_t.shape[0], pt.shape[0] // q_t.shape[0], cache_k_t.shape[-1])
        in_specs, out_specs, out_shape, scratch = (in_specs + extra[0], out_specs + extra[1], out_shape + extra[2],
                                                   extra[3])
        prefetch, operands = [pt], [q_t, cache_k_t]
    outs = pl.pallas_call(
        functools.partial(_ffn_kernel, scan_layer=None if scan is None else scan[3]),
        grid_spec=pltpu.PrefetchScalarGridSpec(num_scalar_prefetch=len(prefetch), grid=(n // rows,),
                                               in_specs=in_specs, out_specs=out_specs, scratch_shapes=scratch),
        out_shape=out_shape,
        compiler_params=pltpu.CompilerParams(dimension_semantics=("arbitrary",),
                                             vmem_limit_bytes=V7X_VMEM_LIMIT_BYTES),
        name="ffn",
    )(*prefetch, x, g, wg, wu, wd, *operands)
    return outs[0] if scan is None else (outs[0], outs[1], _sel_to_blocks(outs[2]))


def _proj(h, win_ref, n):
    d = h.shape[1]
    return _dot(h, win_ref[:, n * d:(n + 1) * d])


def _mix_in_tail(h, win_ref, wbc_ref, gate_b, conv, ga_ref, zc_ref):
    y_conv = _dot((gate_b * conv).astype(BF16), wbc_ref[...])
    ga_ref[...] = jax.nn.sigmoid(_proj(h, win_ref, 6))
    zc_ref[...] = jax.nn.sigmoid(_proj(h, win_ref, 7)) * y_conv


def _mix_in_prompt_kernel(x_ref, g_ref, win_ref, wconv_ref, wbc_ref,
                          q_ref, kt_ref, vt_ref, kb_ref, ksum_ref, ga_ref, zc_ref, cs_ref, cbuf, *, tiles_per_seq):
    i = pl.program_id(0)
    rows, d = x_ref.shape
    h = _rms(x_ref[...], g_ref[...]).astype(BF16)
    q_ref[...] = _proj(h, win_ref, 0)
    k = _proj(h, win_ref, 1)
    kt_ref[0] = k.T
    kb_ref[...] = k.astype(BF16)
    blocks = rows // BLOCK_SIZE
    sums = [jnp.sum(k[j * BLOCK_SIZE:(j + 1) * BLOCK_SIZE], axis=0, keepdims=True) for j in range(blocks)]
    ksum_ref[0, 0] = jnp.concatenate(sums, axis=0)
    vt_ref[0] = _proj(h, win_ref, 2).T
    gate_b = _proj(h, win_ref, 3)
    cu = _proj(h, win_ref, 4) * _proj(h, win_ref, 5)

    @pl.when(i % tiles_per_seq == 0)
    def _():
        cbuf[0:V7X_SUBLANES, :] = jnp.zeros((V7X_SUBLANES, d), F32)

    cbuf[V7X_SUBLANES:V7X_SUBLANES + rows, :] = cu
    w = wconv_ref[...]
    conv = (w[0:1] * cbuf[V7X_SUBLANES - 2:V7X_SUBLANES - 2 + rows, :]
            + w[1:2] * cbuf[V7X_SUBLANES - 1:V7X_SUBLANES - 1 + rows, :]
            + w[2:3] * cu)
    last = cbuf[rows + V7X_SUBLANES - 2:rows + V7X_SUBLANES, :]
    cs_ref[0] = last
    cbuf[V7X_SUBLANES - 2:V7X_SUBLANES, :] = last
    _mix_in_tail(h, win_ref, wbc_ref, gate_b, conv, ga_ref, zc_ref)


def _mix_in_sample_kernel(x_ref, g_ref, win_ref, wconv_ref, wbc_ref, p2_ref, p1_ref,
                          q_ref, k_ref, v_ref, ga_ref, zc_ref, cu_ref):
    h = _rms(x_ref[...], g_ref[...]).astype(BF16)
    q_ref[...] = _proj(h, win_ref, 0)
    k_ref[...] = _proj(h, win_ref, 1)
    v_ref[...] = _proj(h, win_ref, 2)
    gate_b = _proj(h, win_ref, 3)
    cu = _proj(h, win_ref, 4) * _proj(h, win_ref, 5)
    cu_ref[...] = cu
    w = wconv_ref[...]
    conv = w[0:1] * p2_ref[...] + w[1:2] * p1_ref[...] + w[2:3] * cu
    _mix_in_tail(h, win_ref, wbc_ref, gate_b, conv, ga_ref, zc_ref)


def _mix_in_prompt(x, g, win, wconv, wbc, batch, seq, rows):
    n, d = x.shape
    tiles_per_seq = seq // rows
    row_spec = pl.BlockSpec((rows, d), lambda i: (i, 0))
    row_shape = jax.ShapeDtypeStruct((n, d), F32)
    t_spec = pl.BlockSpec((1, d, rows), lambda i: (i // tiles_per_seq, 0, i % tiles_per_seq))
    t_shape = jax.ShapeDtypeStruct((batch, d, seq), F32)
    per_seq = lambda r: pl.BlockSpec((1, r, d), lambda i: (i // tiles_per_seq, 0, 0))
    blocks = rows // BLOCK_SIZE
    return pl.pallas_call(
        functools.partial(_mix_in_prompt_kernel, tiles_per_seq=tiles_per_seq),
        grid=(n // rows,),
        in_specs=[row_spec, _const_spec((1, d)), _const_spec(win.shape), _const_spec(wconv.shape),
                  _const_spec(wbc.shape)],
        out_specs=[row_spec, t_spec, t_spec, row_spec,
                   pl.BlockSpec((1, 1, blocks, d), lambda i: (i // tiles_per_seq, i % tiles_per_seq, 0, 0)),
                   row_spec, row_spec, per_seq(CONV_WIDTH - 1)],
        out_shape=[row_shape, t_shape, t_shape, jax.ShapeDtypeStruct((n, d), BF16),
                   jax.ShapeDtypeStruct((batch, tiles_per_seq, blocks, d), F32), row_shape, row_shape,
                   jax.ShapeDtypeStruct((batch, CONV_WIDTH - 1, d), F32)],
        scratch_shapes=[pltpu.VMEM((rows + V7X_SUBLANES, d), F32)],
        compiler_params=pltpu.CompilerParams(dimension_semantics=("arbitrary",),
                                             vmem_limit_bytes=V7X_VMEM_LIMIT_BYTES),
        name="mix_in_prompt",
    )(x, g, win, wconv, wbc)


def _mix_in_sample(x, g, win, wconv, wbc, prev2, prev1):
    n, d = x.shape
    full = pl.BlockSpec((n, d), lambda i: (0, 0))
    row_shape = jax.ShapeDtypeStruct((n, d), F32)
    return pl.pallas_call(
        _mix_in_sample_kernel,
        grid=(1,),
        in_specs=[full, _const_spec((1, d)), _const_spec(win.shape), _const_spec(wconv.shape),
                  _const_spec(wbc.shape), full, full],
        out_specs=[full] * 6,
        out_shape=[row_shape] * 6,
        compiler_params=pltpu.CompilerParams(dimension_semantics=("arbitrary",),
                                             vmem_limit_bytes=V7X_VMEM_LIMIT_BYTES),
        name="mix_in_sample",
    )(x, g, win, wconv, wbc, prev2, prev1)


def _t5_bucket(dist):
    n = jnp.maximum(dist, 0)
    max_exact = N_BUCKETS // 2
    nf = jnp.maximum(n, 1).astype(F32)
    large = max_exact + (jnp.log(nf / max_exact) / math.log(MAX_DISTANCE / max_exact)
                         * (N_BUCKETS - max_exact)).astype(jnp.int32)
    return jnp.where(n < max_exact, n, jnp.minimum(large, N_BUCKETS - 1))


def _bias_lookup(dist, table_row):
    bucket = _t5_bucket(dist)
    bias = jnp.zeros(dist.shape, F32)
    for b in range(N_BUCKETS):
        bias = jnp.where(bucket == b, table_row(b), bias)
    return bias


def _bias_tiles_kernel(rb_ref, o_ref):
    h = pl.program_id(0)
    kk = lax.broadcasted_iota(jnp.int32, (BLOCK_SIZE, BLOCK_SIZE), 0)
    qq = lax.broadcasted_iota(jnp.int32, (BLOCK_SIZE, BLOCK_SIZE), 1)
    own = _bias_lookup(qq - kk, lambda b: rb_ref[b, h]) * LOG2_E
    o_ref[0, 0] = jnp.where(qq >= kk, own, NEG_INF)
    o_ref[0, 1] = _bias_lookup(qq - kk + BLOCK_SIZE, lambda b: rb_ref[b, h]) * LOG2_E


def _bias_tiles(rel_bias):
    return pl.pallas_call(
        _bias_tiles_kernel,
        grid=(N_HEADS,),
        in_specs=[pl.BlockSpec(memory_space=pltpu.SMEM)],
        out_specs=pl.BlockSpec((1, 2, BLOCK_SIZE, BLOCK_SIZE), lambda h: (h, 0, 0, 0)),
        out_shape=jax.ShapeDtypeStruct((N_HEADS, 2, BLOCK_SIZE, BLOCK_SIZE), F32),
        name="bias_tiles",
    )(rel_bias)


def _bias_last_block_kernel(rb_ref, o_ref):
    h = pl.program_id(0)
    _, n_p, page_size = o_ref.shape
    p = lax.broadcasted_iota(jnp.int32, (n_p, page_size), 0)
    r = lax.broadcasted_iota(jnp.int32, (n_p, page_size), 1)
    o_ref[0] = _bias_lookup(BLOCK_SIZE - (p * page_size + r), lambda b: rb_ref[b, h])


def _bias_last_block(rel_bias, page_size):
    shape = (N_HEADS, BLOCK_SIZE // page_size, page_size)
    return pl.pallas_call(
        _bias_last_block_kernel,
        grid=(N_HEADS,),
        in_specs=[pl.BlockSpec(memory_space=pltpu.SMEM)],
        out_specs=pl.BlockSpec((1,) + shape[1:], lambda h: (h, 0, 0)),
        out_shape=jax.ShapeDtypeStruct(shape, F32),
        name="bias_last_block",
    )(rel_bias)


def _attn_prompt_kernel(rb_ref, q_ref, k_ref, vt_ref, ksum_ref, bt_ref, o_ref, kb_sc, vt_sc, sel_sc):
    hq = pl.program_id(1)
    qb = pl.program_id(2)
    n_blk = k_ref.shape[1] // BLOCK_SIZE
    n_grp = q_ref.shape[2] // V7X_LANES
    n_heads = n_grp * HEADS_PER_GROUP
    bs = BLOCK_SIZE

    @pl.when(qb == 0)
    def _():
        kb_sc[0:bs, :] = jnp.zeros((bs, kb_sc.shape[1]), BF16)
        kb_sc[bs:, :] = k_ref[0]
        sel_sc[:, n_blk:, :] = jnp.zeros((n_heads, sel_sc.shape[1] - n_blk, bs), F32)
        sub = lax.broadcasted_iota(jnp.int32, (V7X_SUBLANES * 2, bs), 0)
        ones_rows = jnp.where(sub == 0, 1.0, 0.0).astype(BF16)
        for g in range(n_grp):
            vt_sc[g, 0] = jnp.zeros(vt_sc.shape[2:], BF16)
            for j in range(n_blk):
                vt_sc[g, j + 1, 0:V7X_LANES, :] = vt_ref[0, g * V7X_LANES:(g + 1) * V7X_LANES,
                                                         j * bs:(j + 1) * bs].astype(BF16)
                vt_sc[g, j + 1, V7X_LANES:, :] = ones_rows

    q = q_ref[0] * (SCALE * LOG2_E)
    lane = lax.broadcasted_iota(jnp.int32, (bs, V7X_LANES), 1)
    blk_row = lax.broadcasted_iota(jnp.int32, (n_blk, bs), 0)
    qms = []
    for c in range(n_heads):
        g, i = divmod(c, HEADS_PER_GROUP)
        lanes = slice(g * V7X_LANES, (g + 1) * V7X_LANES)
        qm = jnp.where(lane // HEAD_DIM == i, q[:, lanes], 0.0).astype(BF16)
        qms.append(qm)
        st = _dot_nt((ksum_ref[0, :, lanes] / BLOCK_SIZE).astype(BF16), qm)
        rank = jnp.zeros(st.shape, jnp.int32)
        for j in range(n_blk):
            sj = st[j:j + 1, :]
            beats = (sj > st) | ((sj == st) & (j < blk_row))
            rank = rank + jnp.where(beats & (j < qb), 1, 0)
        sel_sc[c, 0:n_blk, :] = jnp.where((rank < TOP_K) & (blk_row < qb), 1.0, 0.0)

    def col_max(x):
        return jnp.max(x, axis=0, keepdims=True)

    def window(c, first_padded_block):
        g = c // HEADS_PER_GROUP
        rows = pl.ds(pl.multiple_of(first_padded_block * bs, bs), 2 * bs)
        s = _dot_nt(kb_sc[rows, g * V7X_LANES:(g + 1) * V7X_LANES], qms[c])
        return s[:bs], s[bs:]

    prev_row = jnp.where(qb == 0, n_blk, qb - 1)
    scores = []
    for c in range(n_heads):
        s_prev, s_own = window(c, qb)
        s_prev = s_prev + bt_ref[c, 1] + jnp.where(sel_sc[c, pl.ds(prev_row, 1), :] > 0.5, 0.0, NEG_INF)
        scores.append((s_prev, s_own + bt_ref[c, 0]))
    states = []
    for c in range(n_heads):
        g = c // HEADS_PER_GROUP
        s_prev, s_own = scores[c]
        m = jnp.maximum(col_max(s_prev), col_max(s_own))
        p_prev, p_own = jnp.exp2(s_prev - m), jnp.exp2(s_own - m)
        acc = _dot(vt_sc[g, qb], p_prev.astype(BF16)) + _dot(vt_sc[g, qb + 1], p_own.astype(BF16))
        states.append((m, acc))

    n_far = jnp.maximum(qb - 1, 0)

    def far_body(t, states):
        j0 = 2 * t
        scores = []
        for c in range(n_heads):
            far_bias = rb_ref[N_BUCKETS - 1, hq * n_heads + c] * LOG2_E
            s0, s1 = window(c, j0 + 1)
            s0 = s0 + jnp.where(sel_sc[c, pl.ds(j0, 1), :] > 0.5, far_bias, NEG_INF)
            s1 = s1 + jnp.where((sel_sc[c, pl.ds(j0 + 1, 1), :] > 0.5) & (j0 + 1 < n_far), far_bias, NEG_INF)
            scores.append((s0, s1))
        new = []
        for c in range(n_heads):
            g = c // HEADS_PER_GROUP
            s0, s1 = scores[c]
            m, acc = states[c]
            m_new = jnp.maximum(m, jnp.maximum(col_max(s0), col_max(s1)))
            alpha = jnp.exp2(m - m_new)
            p0, p1 = jnp.exp2(s0 - m_new), jnp.exp2(s1 - m_new)
            acc_new = (alpha * acc + _dot(vt_sc[g, j0 + 1], p0.astype(BF16))
                       + _dot(vt_sc[g, j0 + 2], p1.astype(BF16)))
            new.append((m_new, acc_new))
        return tuple(new)

    states = lax.fori_loop(0, (n_far + 1) // 2, far_body, tuple(states))
    dim_row = lax.broadcasted_iota(jnp.int32, (V7X_LANES, bs), 0)
    for g in range(n_grp):
        out_t = jnp.zeros((V7X_LANES, bs), F32)
        for i in range(HEADS_PER_GROUP):
            m, acc = states[g * HEADS_PER_GROUP + i]
            out_t = jnp.where(dim_row // HEAD_DIM == i, acc[:V7X_LANES] * (1.0 / acc[V7X_LANES:V7X_LANES + 1]), out_t)
        o_ref[0, :, g * V7X_LANES:(g + 1) * V7X_LANES] = out_t.T


def _attn_prompt(rel_bias, q, k_bf16, v_t, k_sums, bias_tiles):
    batch, seq, d = q.shape
    n_blk = seq // BLOCK_SIZE
    width = ATTN_HEADS * HEAD_DIM
    n_grp = width // V7X_LANES
    q_spec = pl.BlockSpec((1, BLOCK_SIZE, width), lambda b, g, i: (b, i, g))
    return pl.pallas_call(
        _attn_prompt_kernel,
        grid=(batch, d // width, n_blk),
        in_specs=[pl.BlockSpec(memory_space=pltpu.SMEM), q_spec,
                  pl.BlockSpec((1, seq, width), lambda b, g, i: (b, 0, g)),
                  pl.BlockSpec((1, width, seq), lambda b, g, i: (b, g, 0)),
                  pl.BlockSpec((1, n_blk, width), lambda b, g, i: (b, 0, g)),
                  pl.BlockSpec((ATTN_HEADS, 2, BLOCK_SIZE, BLOCK_SIZE), lambda b, g, i: (g, 0, 0, 0))],
        out_specs=q_spec,
        out_shape=jax.ShapeDtypeStruct((batch, seq, d), F32),
        scratch_shapes=[pltpu.VMEM((seq + BLOCK_SIZE, width), BF16),
                        pltpu.VMEM((n_grp, n_blk + 1, V7X_LANES + 2 * V7X_SUBLANES, BLOCK_SIZE), BF16),
                        pltpu.VMEM((ATTN_HEADS, 2 * V7X_SUBLANES, BLOCK_SIZE), F32)],
        compiler_params=pltpu.CompilerParams(dimension_semantics=("arbitrary", "arbitrary", "arbitrary"),
                                             vmem_limit_bytes=V7X_VMEM_LIMIT_BYTES),
        name="attn_prompt",
    )(rel_bias, q, k_bf16, v_t, k_sums, bias_tiles)


def _scan_keys_kernel(pt_ref, qt_ref, ck_ref, lg_ref, sel_ref, pages, qb_sc, pg_sc, sem, *, layer, n_groups):
    b = pl.program_id(0)
    g = pl.program_id(1)
    step = b * n_groups + g
    n_steps = pl.num_programs(0) * n_groups
    page_size = ck_ref.shape[-1]
    pages_per_block = BLOCK_SIZE // page_size
    n_blocks = n_groups * SCAN_PAGES // pages_per_block

    def copies(step, slot):
        return [pltpu.make_async_copy(ck_ref.at[layer, pt_ref[step * SCAN_PAGES + i]], pages.at[slot, i],
                                      sem.at[slot]) for i in range(SCAN_PAGES)]

    slot = step % 2

    @pl.when(step == 0)
    def _():
        for c in copies(0, 0):
            c.start()

    @pl.when(step + 1 < n_steps)
    def _():
        for c in copies(step + 1, 1 - slot):
            c.start()

    @pl.when(g == 0)
    def _():
        qt = qt_ref[0]
        for h in range(N_HEADS):
            qb_sc[h] = jnp.broadcast_to(qt[:, h:h + 1], (HEAD_DIM, page_size))

    for c in copies(step, slot):
        c.wait()

    def page_body(i, carry):
        for h in range(N_HEADS):
            lg_ref[0, h, pl.ds(i, 1), :] = jnp.sum(pages[slot, i, h] * qb_sc[h], axis=0, keepdims=True)
        return carry

    lax.fori_loop(0, SCAN_PAGES, page_body, 0)

    pg_sc[pl.ds(pl.multiple_of(g * SCAN_PAGES, SCAN_PAGES), SCAN_PAGES), :] = _page_sums(lg_ref, SCAN_PAGES)

    @pl.when(g == n_groups - 1)
    def _():
        _topk_blocks(pg_sc, sel_ref, n_blocks, pages_per_block)


def _scan_keys(page_table_flat, q_t, cache_k_t, layer, n_pages):
    n_seq = q_t.shape[0]
    page_size = cache_k_t.shape[-1]
    n_groups = n_pages // SCAN_PAGES
    logits, sel = pl.pallas_call(
        functools.partial(_scan_keys_kernel, layer=layer, n_groups=n_groups),
        grid_spec=pltpu.PrefetchScalarGridSpec(
            num_scalar_prefetch=1,
            grid=(n_seq, n_groups),
            in_specs=[pl.BlockSpec((1, HEAD_DIM, N_HEADS), lambda b, g, pt: (b, 0, 0)),
                      pl.BlockSpec(memory_space=pl.ANY)],
            out_specs=[pl.BlockSpec((1, N_HEADS, SCAN_PAGES, page_size), lambda b, g, pt: (b, 0, g, 0)),
                       pl.BlockSpec((1, V7X_SUBLANES, V7X_LANES), lambda b, g, pt: (b, 0, 0))],
            scratch_shapes=[pltpu.VMEM((2, SCAN_PAGES, N_HEADS, HEAD_DIM, page_size), F32),
                            pltpu.VMEM((N_HEADS, HEAD_DIM, page_size), F32),
                            pltpu.VMEM((n_pages, V7X_LANES), F32),
                            pltpu.SemaphoreType.DMA((2,))]),
        out_shape=[jax.ShapeDtypeStruct((n_seq, N_HEADS, n_pages, page_size), F32),
                   jax.ShapeDtypeStruct((n_seq, V7X_SUBLANES, V7X_LANES), jnp.int32)],
        compiler_params=pltpu.CompilerParams(dimension_semantics=("arbitrary", "arbitrary"),
                                             vmem_limit_bytes=V7X_VMEM_LIMIT_BYTES),
        name="scan_keys",
    )(page_table_flat, q_t, cache_k_t)
    return logits, _sel_to_blocks(sel)


def _attn_sample_kernel(pt_ref, sel_ref, rb_ref, lg_ref, q_ref, kn_ref, vnt_ref, bl_ref, cv_ref, ot_ref,
                        vbuf, sem, *, layer, n_pages):
    b = pl.program_id(0)
    n_seq = pl.num_programs(0)
    page_size = cv_ref.shape[-1]
    pages_per_block = BLOCK_SIZE // page_size
    last_block = n_pages // pages_per_block - 1

    def selected(seq, h, s):
        return sel_ref[(seq * N_HEADS + h) * TOP_K + s]

    def copies(seq, slot):
        out = []
        for h in range(N_HEADS):
            for s in range(TOP_K):
                blk = selected(seq, h, s)
                for p in range(pages_per_block):
                    phys = pt_ref[seq * n_pages + blk * pages_per_block + p]
                    out.append(pltpu.make_async_copy(cv_ref.at[layer, phys, h],
                                                     vbuf.at[slot, h, s * pages_per_block + p], sem.at[slot]))
        return out

    slot = b % 2

    @pl.when(b == 0)
    def _():
        for c in copies(0, 0):
            c.start()

    @pl.when(b + 1 < n_seq)
    def _():
        for c in copies(b + 1, 1 - slot):
            c.start()

    for c in copies(b, slot):
        c.wait()

    def all_max(x):
        return jnp.max(jnp.max(x, axis=1, keepdims=True), axis=0, keepdims=True)

    def all_sum(x):
        return jnp.sum(jnp.sum(x, axis=1, keepdims=True), axis=0, keepdims=True)

    new_dots = jnp.sum(q_ref[0] * kn_ref[0], axis=1, keepdims=True) * SCALE
    for h in range(N_HEADS):
        logit_new = new_dots[h:h + 1, :] + rb_ref[0, h]
        pieces = []
        for s in range(TOP_K):
            blk = selected(b, h, s)
            raw = lg_ref[0, h, pl.ds(pl.multiple_of(blk * pages_per_block, pages_per_block), pages_per_block), :]
            bias = jnp.where(blk == last_block, bl_ref[h], rb_ref[N_BUCKETS - 1, h])
            pieces.append(raw * SCALE + bias)
        m = logit_new
        for x in pieces:
            m = jnp.maximum(m, all_max(x))
        p_new = jnp.exp(logit_new - m)
        denom = p_new
        acc = None
        for s, x in enumerate(pieces):
            p = jnp.exp(x - m)
            denom = denom + all_sum(p)
            for pg in range(pages_per_block):
                term = p[pg:pg + 1, :] * vbuf[slot, h, s * pages_per_block + pg]
                acc = term if acc is None else acc + term
        num = jnp.sum(acc, axis=1, keepdims=True) + p_new * vnt_ref[0, :, h:h + 1]
        ot_ref[0, :, h:h + 1] = num / denom


def _attn_sample(page_table_flat, sel_flat, rel_bias, logits, q, k_new, v_new_t, bias_last, cache_v_t, layer,
                 n_pages):
    n_seq = q.shape[0]
    page_size = cache_v_t.shape[-1]
    pages_per_block = BLOCK_SIZE // page_size
    seq_spec = pl.BlockSpec((1, N_HEADS, HEAD_DIM), lambda b, pt, sel: (b, 0, 0))
    seq_t_spec = pl.BlockSpec((1, HEAD_DIM, N_HEADS), lambda b, pt, sel: (b, 0, 0))
    return pl.pallas_call(
        functools.partial(_attn_sample_kernel, layer=layer, n_pages=n_pages),
        grid_spec=pltpu.PrefetchScalarGridSpec(
            num_scalar_prefetch=2,
            grid=(n_seq,),
            in_specs=[pl.BlockSpec(memory_space=pltpu.SMEM),
                      pl.BlockSpec((1, N_HEADS, n_pages, page_size), lambda b, pt, sel: (b, 0, 0, 0)),
                      seq_spec, seq_spec, seq_t_spec,
                      pl.BlockSpec(bias_last.shape, lambda b, pt, sel: (0, 0, 0)),
                      pl.BlockSpec(memory_space=pl.ANY)],
            out_specs=seq_t_spec,
            scratch_shapes=[pltpu.VMEM((2, N_HEADS, TOP_K * pages_per_block, HEAD_DIM, page_size), F32),
                            pltpu.SemaphoreType.DMA((2,))]),
        out_shape=jax.ShapeDtypeStruct((n_seq, HEAD_DIM, N_HEADS), F32),
        compiler_params=pltpu.CompilerParams(dimension_semantics=("arbitrary",),
                                             vmem_limit_bytes=V7X_VMEM_LIMIT_BYTES),
        name="attn_sample",
    )(page_table_flat, sel_flat, rel_bias, logits, q, k_new, v_new_t, bias_last, cache_v_t)


def _mix_out_kernel(*refs, final, scan_layer):
    if scan_layer is None:
        x_ref, oa_ref, ga_ref, zc_ref, wba_ref, wout_ref, g2_ref, wg_ref, wu_ref, wd_ref, gf_ref, y_ref = refs
        side_work = finish = None
    else:
        (pt_ref, x_ref, oa_ref, ga_ref, zc_ref, wba_ref, wout_ref, g2_ref, wg_ref, wu_ref, wd_ref, gf_ref,
         qt_ref, ck_ref, y_ref, lg_ref, sel_ref) = refs[:17]
        side_work, finish = _hosted_scan(pt_ref, qt_ref, ck_ref, lg_ref, sel_ref, *refs[17:], layer=scan_layer)
    y_attn = _dot(oa_ref[...].astype(BF16), wba_ref[...])
    merged = ga_ref[...] * y_attn + zc_ref[...]
    x = x_ref[...] + _dot(merged.astype(BF16), wout_ref[...])
    h = _rms(x, g2_ref[...]).astype(BF16)
    x = x + 0.5 * _swiglu(h, wg_ref, wu_ref, wd_ref, side_work)
    y_ref[...] = _rms(x, gf_ref[...]) if final else x
    if finish is not None:
        finish()


def _mix_out(x, oa, ga, zc, wba, wout, g2, wg, wu, wd, gf, rows, final, scan=None):
    n, d = x.shape
    row_spec = pl.BlockSpec((rows, d), lambda i, *_: (i, 0))
    in_specs = [row_spec] * 4 + [_const_spec(wba.shape), _const_spec(wout.shape), _const_spec((1, d)),
                                 _const_spec(wg.shape), _const_spec(wu.shape), _const_spec(wd.shape),
                                 _const_spec((1, d))]
    out_specs, out_shape, scratch, prefetch, operands = [row_spec], [jax.ShapeDtypeStruct((n, d), F32)], [], [], []
    if scan is not None:
        pt, q_t, cache_k_t, layer = scan
        assert q_t.shape[0] == n // rows
        extra = _hosted_scan_specs(q_t.shape[0], pt.shape[0] // q_t.shape[0], cache_k_t.shape[-1])
        in_specs, out_specs, out_shape, scratch = (in_specs + extra[0], out_specs + extra[1], out_shape + extra[2],
                                                   extra[3])
        prefetch, operands = [pt], [q_t, cache_k_t]
    outs = pl.pallas_call(
        functools.partial(_mix_out_kernel, final=final, scan_layer=None if scan is None else scan[3]),
        grid_spec=pltpu.PrefetchScalarGridSpec(num_scalar_prefetch=len(prefetch), grid=(n // rows,),
                                               in_specs=in_specs, out_specs=out_specs, scratch_shapes=scratch),
        out_shape=out_shape,
        compiler_params=pltpu.CompilerParams(dimension_semantics=("arbitrary",),
                                             vmem_limit_bytes=V7X_VMEM_LIMIT_BYTES),
        name="mix_out",
    )(*prefetch, x, oa, ga, zc, wba, wout, g2, wg, wu, wd, gf, *operands)
    return outs[0] if scan is None else (outs[0], outs[1], _sel_to_blocks(outs[2]))


def kernel(x_prompt, x_sample, cache_k, cache_v, state_conv, page_table, rel_bias, norm_ffn1, w_ffn1_gate,
           w_ffn1_up, w_ffn1_down, norm_mix, w_in, w_conv, w_branch_attn, w_branch_conv, w_out, norm_ffn2,
           w_ffn2_gate, w_ffn2_up, w_ffn2_down, norm_final):
    batch, seq, d = x_prompt.shape
    n_seq, t_new, _ = x_sample.shape
    depth = w_in.shape[0]
    n_pages = page_table.shape[1]
    page_size = cache_k.shape[2]
    assert t_new == 1 and d == N_HEADS * HEAD_DIM and w_in.shape[2] == 8 * d
    assert seq % PROMPT_ROWS == 0 and PROMPT_ROWS % BLOCK_SIZE == 0 and N_HEADS % ATTN_HEADS == 0
    assert (n_pages * page_size) % BLOCK_SIZE == 0 and n_pages % SCAN_PAGES == 0 and BLOCK_SIZE % page_size == 0
    assert MAX_DISTANCE <= BLOCK_SIZE and n_pages % SCAN_RING == 0 and seq % MIX_OUT_ROWS == 0
    assert (batch * seq) // PROMPT_ROWS + (batch * seq) // MIX_OUT_ROWS < n_seq

    xp = x_prompt.reshape(batch * seq, d)
    xs = x_sample.reshape(n_seq, d)
    pt_flat = page_table.reshape(-1)
    cache_k_t = jnp.transpose(cache_k, (0, 1, 3, 4, 2))
    cache_v_t = jnp.transpose(cache_v, (0, 1, 3, 4, 2))
    bias_tiles = _bias_tiles(rel_bias)
    bias_last = _bias_last_block(rel_bias, page_size)
    outs = {name: [] for name in ("kp", "vp", "cp", "ks", "vs", "cs")}
    for layer in range(depth):
        g1, gm, g2 = (g[layer].reshape(1, d) for g in (norm_ffn1, norm_mix, norm_ffn2))
        gf = norm_final.reshape(1, d)
        wg1, wu1, wd1, win, wba, wbc, wout, wg2, wu2, wd2 = (
            w[layer].astype(BF16) for w in (w_ffn1_gate, w_ffn1_up, w_ffn1_down, w_in, w_branch_attn,
                                            w_branch_conv, w_out, w_ffn2_gate, w_ffn2_up, w_ffn2_down))
        wconv = w_conv[layer]
        final = layer == depth - 1

        xs = _ffn(xs, g1, wg1, wu1, wd1, n_seq)
        prev2, prev1 = state_conv[layer, :, 0, :], state_conv[layer, :, 1, :]
        qs, ks, vs, gas, zcs, cu = _mix_in_sample(xs, gm, win, wconv, wbc, prev2, prev1)
        q3, k3, v3 = (a.reshape(n_seq, N_HEADS, HEAD_DIM) for a in (qs, ks, vs))
        q3_t, v3_t = jnp.transpose(q3, (0, 2, 1)), jnp.transpose(v3, (0, 2, 1))
        bounds = [0, (batch * seq) // PROMPT_ROWS, (batch * seq) // PROMPT_ROWS + (batch * seq) // MIX_OUT_ROWS, n_seq]
        pages = lambda r: pt_flat[bounds[r] * n_pages:bounds[r + 1] * n_pages]
        hosted = lambda r: (pages(r), q3_t[bounds[r]:bounds[r + 1]], cache_k_t, layer)

        xp, *scan0 = _ffn(xp, g1, wg1, wu1, wd1, PROMPT_ROWS, hosted(0))
        q, k_t, v_t, k_bf16, k_sums, ga, zc, conv_state = _mix_in_prompt(xp, gm, win, wconv, wbc, batch, seq,
                                                                        PROMPT_ROWS)
        oa = _attn_prompt(rel_bias, q.reshape(batch, seq, d), k_bf16.reshape(batch, seq, d), v_t,
                          k_sums.reshape(batch, seq // BLOCK_SIZE, d), bias_tiles).reshape(batch * seq, d)
        xp, *scan1 = _mix_out(xp, oa, ga, zc, wba, wout, g2, wg2, wu2, wd2, gf, MIX_OUT_ROWS, final, hosted(1))
        for name, a_t in (("kp", k_t), ("vp", v_t)):
            outs[name].append(jnp.transpose(a_t.reshape(batch, N_HEADS, HEAD_DIM, seq), (0, 3, 1, 2)))
        outs["cp"].append(conv_state)

        scan2 = _scan_keys(pages(2), q3_t[bounds[2]:], cache_k_t, layer, n_pages)
        oa_parts = []
        for r, (logits, sel) in enumerate((scan0, scan1, scan2)):
            lo, hi = bounds[r], bounds[r + 1]
            oa_parts.append(_attn_sample(pages(r), sel.reshape(-1), rel_bias, logits, q3[lo:hi], k3[lo:hi],
                                         v3_t[lo:hi], bias_last, cache_v_t, layer, n_pages))
        oa = jnp.transpose(jnp.concatenate(oa_parts, axis=0), (0, 2, 1)).reshape(n_seq, d)
        xs = _mix_out(xs, oa, gas, zcs, wba, wout, g2, wg2, wu2, wd2, gf, n_seq, final)
        outs["ks"].append(k3.reshape(n_seq, 1, N_HEADS, HEAD_DIM))
        outs["vs"].append(v3.reshape(n_seq, 1, N_HEADS, HEAD_DIM))
        outs["cs"].append(jnp.stack([prev1, cu], axis=1))

    return (xp.reshape(batch, seq, d), xs.reshape(n_seq, 1, d), jnp.stack(outs["kp"]), jnp.stack(outs["vp"]),
            jnp.stack(outs["cp"]), jnp.stack(outs["ks"]), jnp.stack(outs["vs"]), jnp.stack(outs["cs"]))
```

```python
import functools
import math

import jax
import jax.numpy as jnp
from jax import lax
from jax.experimental import pallas as pl
from jax.experimental.pallas import tpu as pltpu

N_HEADS = 16
HEAD_DIM = 64
BLOCK_SIZE = 256
TOP_K = 3
N_BUCKETS = 32
MAX_DISTANCE = 128
CONV_WIDTH = 3
RMS_EPS = 1e-6
SCALE = HEAD_DIM ** -0.5
LOG2_E = math.log2(math.e)

V7X_LANES = 128
V7X_SUBLANES = 8
V7X_VMEM_LIMIT_BYTES = 60 * 1024 * 1024

HEADS_PER_GROUP = V7X_LANES // HEAD_DIM
FF_CHUNK = 256
PROMPT_ROWS = 512
MIX_OUT_ROWS = 256
ATTN_HEADS = 8
SCAN_PAGES = 16
SCAN_RING = 16

F32 = jnp.float32
BF16 = jnp.bfloat16
NEG_INF = float("-inf")


def _const_spec(shape):
    nd = len(shape)
    return pl.BlockSpec(shape, lambda *_: (0,) * nd, pipeline_mode=pl.Buffered(1))


def _rms(x, g):
    ms = jnp.mean(x * x, axis=-1, keepdims=True)
    return (x * lax.rsqrt(ms + RMS_EPS)) * g


def _dot(a, b):
    return jnp.dot(a, b, preferred_element_type=F32)


def _dot_nt(a, b):
    return lax.dot_general(a, b, (((1,), (1,)), ((), ())), preferred_element_type=F32)


def _swiglu(h, wg_ref, wu_ref, wd_ref, side_work=None):
    d_ff = wg_ref.shape[1]
    n_chunks = d_ff // FF_CHUNK
    acc = None
    for c in range(n_chunks):
        sl = slice(c * FF_CHUNK, (c + 1) * FF_CHUNK)
        if side_work is not None:
            side_work(c, n_chunks, 0)
        gate = _dot(h, wg_ref[:, sl])
        if side_work is not None:
            side_work(c, n_chunks, 1)
        up = _dot(h, wu_ref[:, sl])
        if side_work is not None:
            side_work(c, n_chunks, 2)
        act = (gate * jax.nn.sigmoid(gate) * up).astype(BF16)
        part = _dot(act, wd_ref[sl, :])
        acc = part if acc is None else acc + part
        if side_work is not None:
            side_work(c, n_chunks, 3)
    return acc


def _topk_blocks(pg_sc, sel_ref, n_blocks, pages_per_block):
    cur = pg_sc[pl.ds(0, n_blocks, stride=pages_per_block), :]
    for p in range(1, pages_per_block):
        cur = cur + pg_sc[pl.ds(p, n_blocks, stride=pages_per_block), :]
    cur = cur / BLOCK_SIZE
    row = lax.broadcasted_iota(jnp.int32, cur.shape, 0)
    out_row = lax.broadcasted_iota(jnp.int32, sel_ref.shape[1:], 0)
    out = jnp.zeros(sel_ref.shape[1:], jnp.int32)
    for t in range(TOP_K):
        best = jnp.max(cur, axis=0, keepdims=True)
        idx = jnp.min(jnp.where(cur == best, row, n_blocks), axis=0, keepdims=True)
        out = jnp.where(out_row == t, idx, out)
        cur = jnp.where(row == idx, NEG_INF, cur)
    sel_ref[0] = out


def _page_sums(lg_ref, n_rows):
    lane = lax.broadcasted_iota(jnp.int32, (n_rows, V7X_LANES), 1)
    tile = jnp.zeros((n_rows, V7X_LANES), F32)
    for h in range(N_HEADS):
        tile = jnp.where(lane == h, jnp.sum(lg_ref[0, h], axis=1, keepdims=True), tile)
    return tile


def _hosted_scan(pt_ref, qt_ref, ck_ref, lg_ref, sel_ref, ring, qb_sc, pg_sc, sem, *, layer):
    step = pl.program_id(0)
    n_steps = pl.num_programs(0)
    n_pages, page_size = lg_ref.shape[2:]
    last_page = n_steps * n_pages - 1

    def copy(pg, ahead=0):
        n = jnp.minimum(step * n_pages + pg + ahead, last_page)
        slot = (pg + ahead) % SCAN_RING
        return pltpu.make_async_copy(ck_ref.at[layer, pt_ref[n]], ring.at[slot], sem.at[slot])

    @pl.when(step == 0)
    def _():
        for pg in range(SCAN_RING):
            copy(pg).start()

    qt = qt_ref[0]
    for h in range(N_HEADS):
        qb_sc[h] = jnp.broadcast_to(qt[:, h:h + 1], (HEAD_DIM, page_size))

    def side_work(c, n_chunks, phase):
        per_chunk = -(-n_pages // n_chunks)
        pages = range(c * per_chunk, min(n_pages, (c + 1) * per_chunk))
        if phase == 0:
            for pg in pages:
                copy(pg).wait()
            return
        share = -(-len(pages) // 3)
        for h in range(N_HEADS):
            q_h = qb_sc[h]
            for pg in pages[(phase - 1) * share:phase * share]:
                lg_ref[0, h, pg:pg + 1, :] = jnp.sum(ring[pg % SCAN_RING, h] * q_h, axis=0, keepdims=True)
        if phase == 3:
            for pg in pages:
                copy(pg, SCAN_RING).start()

    def finish():
        pg_sc[...] = _page_sums(lg_ref, n_pages)
        _topk_blocks(pg_sc, sel_ref, n_pages * page_size // BLOCK_SIZE, BLOCK_SIZE // page_size)

        @pl.when(step == n_steps - 1)
        def _():
            for pg in range(n_pages - SCAN_RING, n_pages):
                copy(pg, SCAN_RING).wait()

    return side_work, finish


def _hosted_scan_specs(n_host, n_pages, page_size):
    in_specs = [pl.BlockSpec((1, HEAD_DIM, N_HEADS), lambda i, *_: (i, 0, 0)), pl.BlockSpec(memory_space=pl.ANY)]
    out_specs = [pl.BlockSpec((1, N_HEADS, n_pages, page_size), lambda i, *_: (i, 0, 0, 0)),
                 pl.BlockSpec((1, V7X_SUBLANES, V7X_LANES), lambda i, *_: (i, 0, 0))]
    out_shape = [jax.ShapeDtypeStruct((n_host, N_HEADS, n_pages, page_size), F32),
                 jax.ShapeDtypeStruct((n_host, V7X_SUBLANES, V7X_LANES), jnp.int32)]
    scratch = [pltpu.VMEM((SCAN_RING, N_HEADS, HEAD_DIM, page_size), F32),
               pltpu.VMEM((N_HEADS, HEAD_DIM, page_size), F32),
               pltpu.VMEM((n_pages, V7X_LANES), F32),
               pltpu.SemaphoreType.DMA((SCAN_RING,))]
    return in_specs, out_specs, out_shape, scratch


def _sel_to_blocks(sel):
    return jnp.transpose(sel[:, :TOP_K, :N_HEADS], (0, 2, 1))


def _ffn_kernel(*refs, scan_layer):
    if scan_layer is None:
        x_ref, g_ref, wg_ref, wu_ref, wd_ref, o_ref = refs
        side_work = finish = None
    else:
        pt_ref, x_ref, g_ref, wg_ref, wu_ref, wd_ref, qt_ref, ck_ref, o_ref, lg_ref, sel_ref = refs[:11]
        side_work, finish = _hosted_scan(pt_ref, qt_ref, ck_ref, lg_ref, sel_ref, *refs[11:], layer=scan_layer)
    x = x_ref[...]
    h = _rms(x, g_ref[...]).astype(BF16)
    o_ref[...] = x + 0.5 * _swiglu(h, wg_ref, wu_ref, wd_ref, side_work)
    if finish is not None:
        finish()


def _ffn(x, g, wg, wu, wd, rows, scan=None):
    n, d = x.shape
    d_ff = wg.shape[1]
    row_spec = pl.BlockSpec((rows, d), lambda i, *_: (i, 0))
    in_specs = [row_spec, _const_spec((1, d)), _const_spec((d, d_ff)), _const_spec((d, d_ff)), _const_spec((d_ff, d))]
    out_specs, out_shape, scratch, prefetch, operands = [row_spec], [jax.ShapeDtypeStruct((n, d), F32)], [], [], []
    if scan is not None:
        pt, q_t, cache_k_t, layer = scan
        assert q_t.shape[0] == n // rows
        extra = _hosted_scan_specs(q_t.shape[0], pt.shape[0] // q_t.shape[0], cache_k_t.shape[-1])
        in_specs, out_specs, out_shape, scratch = (in_specs + extra[0], out_specs + extra[1], out_shape + extra[2],
                                                   extra[3])
        prefetch, operands = [pt], [q_t, cache_k_t]
    outs = pl.pallas_call(
        functools.partial(_ffn_kernel, scan_layer=None if scan is None else scan[3]),
        grid_spec=pltpu.PrefetchScalarGridSpec(num_scalar_prefetch=len(prefetch), grid=(n // rows,),
                                               in_specs=in_specs, out_specs=out_specs, scratch_shapes=scratch),
        out_shape=out_shape,
        compiler_params=pltpu.CompilerParams(dimension_semantics=("arbitrary",),
                                             vmem_limit_bytes=V7X_VMEM_LIMIT_BYTES),
        name="ffn",
    )(*prefetch, x, g, wg, wu, wd, *operands)
    return outs[0] if scan is None else (outs[0], outs[1], _sel_to_blocks(outs[2]))


def _proj(h, win_ref, n):
    d = h.shape[1]
    return _dot(h, win_ref[:, n * d:(n + 1) * d])


def _mix_in_tail(h, win_ref, wbc_ref, gate_b, conv, ga_ref, zc_ref):
    y_conv = _dot((gate_b * conv).astype(BF16), wbc_ref[...])
    ga_ref[...] = jax.nn.sigmoid(_proj(h, win_ref, 6))
    zc_ref[...] = jax.nn.sigmoid(_proj(h, win_ref, 7)) * y_conv


def _mix_in_prompt_kernel(x_ref, g_ref, win_ref, wconv_ref, wbc_ref,
                          q_ref, kt_ref, vt_ref, kb_ref, ksum_ref, ga_ref, zc_ref, cs_ref, cbuf, *, tiles_per_seq):
    i = pl.program_id(0)
    rows, d = x_ref.shape
    h = _rms(x_ref[...], g_ref[...]).astype(BF16)
    q_ref[...] = _proj(h, win_ref, 0)
    k = _proj(h, win_ref, 1)
    kt_ref[0] = k.T
    kb_ref[...] = k.astype(BF16)
    blocks = rows // BLOCK_SIZE
    sums = [jnp.sum(k[j * BLOCK_SIZE:(j + 1) * BLOCK_SIZE], axis=0, keepdims=True) for j in range(blocks)]
    ksum_ref[0, 0] = jnp.concatenate(sums, axis=0)
    vt_ref[0] = _proj(h, win_ref, 2).T
    gate_b = _proj(h, win_ref, 3)
    cu = _proj(h, win_ref, 4) * _proj(h, win_ref, 5)

    @pl.when(i % tiles_per_seq == 0)
    def _():
        cbuf[0:V7X_SUBLANES, :] = jnp.zeros((V7X_SUBLANES, d), F32)

    cbuf[V7X_SUBLANES:V7X_SUBLANES + rows, :] = cu
    w = wconv_ref[...]
    conv = (w[0:1] * cbuf[V7X_SUBLANES - 2:V7X_SUBLANES - 2 + rows, :]
            + w[1:2] * cbuf[V7X_SUBLANES - 1:V7X_SUBLANES - 1 + rows, :]
            + w[2:3] * cu)
    last = cbuf[rows + V7X_SUBLANES - 2:rows + V7X_SUBLANES, :]
    cs_ref[0] = last
    cbuf[V7X_SUBLANES - 2:V7X_SUBLANES, :] = last
    _mix_in_tail(h, win_ref, wbc_ref, gate_b, conv, ga_ref, zc_ref)


def _mix_in_sample_kernel(x_ref, g_ref, win_ref, wconv_ref, wbc_ref, p2_ref, p1_ref,
                          q_ref, k_ref, v_ref, ga_ref, zc_ref, cu_ref):
    h = _rms(x_ref[...], g_ref[...]).astype(BF16)
    q_ref[...] = _proj(h, win_ref, 0)
    k_ref[...] = _proj(h, win_ref, 1)
    v_ref[...] = _proj(h, win_ref, 2)
    gate_b = _proj(h, win_ref, 3)
    cu = _proj(h, win_ref, 4) * _proj(h, win_ref, 5)
    cu_ref[...] = cu
    w = wconv_ref[...]
    conv = w[0:1] * p2_ref[...] + w[1:2] * p1_ref[...] + w[2:3] * cu
    _mix_in_tail(h, win_ref, wbc_ref, gate_b, conv, ga_ref, zc_ref)


def _mix_in_prompt(x, g, win, wconv, wbc, batch, seq, rows):
    n, d = x.shape
    tiles_per_seq = seq // rows
    row_spec = pl.BlockSpec((rows, d), lambda i: (i, 0))
    row_shape = jax.ShapeDtypeStruct((n, d), F32)
    t_spec = pl.BlockSpec((1, d, rows), lambda i: (i // tiles_per_seq, 0, i % tiles_per_seq))
    t_shape = jax.ShapeDtypeStruct((batch, d, seq), F32)
    per_seq = lambda r: pl.BlockSpec((1, r, d), lambda i: (i // tiles_per_seq, 0, 0))
    blocks = rows // BLOCK_SIZE
    return pl.pallas_call(
        functools.partial(_mix_in_prompt_kernel, tiles_per_seq=tiles_per_seq),
        grid=(n // rows,),
        in_specs=[row_spec, _const_spec((1, d)), _const_spec(win.shape), _const_spec(wconv.shape),
                  _const_spec(wbc.shape)],
        out_specs=[row_spec, t_spec, t_spec, row_spec,
                   pl.BlockSpec((1, 1, blocks, d), lambda i: (i // tiles_per_seq, i % tiles_per_seq, 0, 0)),
                   row_spec, row_spec, per_seq(CONV_WIDTH - 1)],
        out_shape=[row_shape, t_shape, t_shape, jax.ShapeDtypeStruct((n, d), BF16),
                   jax.ShapeDtypeStruct((batch, tiles_per_seq, blocks, d), F32), row_shape, row_shape,
                   jax.ShapeDtypeStruct((batch, CONV_WIDTH - 1, d), F32)],
        scratch_shapes=[pltpu.VMEM((rows + V7X_SUBLANES, d), F32)],
        compiler_params=pltpu.CompilerParams(dimension_semantics=("arbitrary",),
                                             vmem_limit_bytes=V7X_VMEM_LIMIT_BYTES),
        name="mix_in_prompt",
    )(x, g, win, wconv, wbc)


def _mix_in_sample(x, g, win, wconv, wbc, prev2, prev1):
    n, d = x.shape
    full = pl.BlockSpec((n, d), lambda i: (0, 0))
    row_shape = jax.ShapeDtypeStruct((n, d), F32)
    return pl.pallas_call(
        _mix_in_sample_kernel,
        grid=(1,),
        in_specs=[full, _const_spec((1, d)), _const_spec(win.shape), _const_spec(wconv.shape),
                  _const_spec(wbc.shape), full, full],
        out_specs=[full] * 6,
        out_shape=[row_shape] * 6,
        compiler_params=pltpu.CompilerParams(dimension_semantics=("arbitrary",),
                                             vmem_limit_bytes=V7X_VMEM_LIMIT_BYTES),
        name="mix_in_sample",
    )(x, g, win, wconv, wbc, prev2, prev1)


def _t5_bucket(dist):
    n = jnp.maximum(dist, 0)
    max_exact = N_BUCKETS // 2
    nf = jnp.maximum(n, 1).astype(F32)
    large = max_exact + (jnp.log(nf / max_exact) / math.log(MAX_DISTANCE / max_exact)
                         * (N_BUCKETS - max_exact)).astype(jnp.int32)
    return jnp.where(n < max_exact, n, jnp.minimum(large, N_BUCKETS - 1))


def _bias_lookup(dist, table_row):
    bucket = _t5_bucket(dist)
    bias = jnp.zeros(dist.shape, F32)
    for b in range(N_BUCKETS):
        bias = jnp.where(bucket == b, table_row(b), bias)
    return bias


def _bias_tiles_kernel(rb_ref, o_ref):
    h = pl.program_id(0)
    kk = lax.broadcasted_iota(jnp.int32, (BLOCK_SIZE, BLOCK_SIZE), 0)
    qq = lax.broadcasted_iota(jnp.int32, (BLOCK_SIZE, BLOCK_SIZE), 1)
    own = _bias_lookup(qq - kk, lambda b: rb_ref[b, h]) * LOG2_E
    o_ref[0, 0] = jnp.where(qq >= kk, own, NEG_INF)
    o_ref[0, 1] = _bias_lookup(qq - kk + BLOCK_SIZE, lambda b: rb_ref[b, h]) * LOG2_E


def _bias_tiles(rel_bias):
    return pl.pallas_call(
        _bias_tiles_kernel,
        grid=(N_HEADS,),
        in_specs=[pl.BlockSpec(memory_space=pltpu.SMEM)],
        out_specs=pl.BlockSpec((1, 2, BLOCK_SIZE, BLOCK_SIZE), lambda h: (h, 0, 0, 0)),
        out_shape=jax.ShapeDtypeStruct((N_HEADS, 2, BLOCK_SIZE, BLOCK_SIZE), F32),
        name="bias_tiles",
    )(rel_bias)


def _bias_last_block_kernel(rb_ref, o_ref):
    h = pl.program_id(0)
    _, n_p, page_size = o_ref.shape
    p = lax.broadcasted_iota(jnp.int32, (n_p, page_size), 0)
    r = lax.broadcasted_iota(jnp.int32, (n_p, page_size), 1)
    o_ref[0] = _bias_lookup(BLOCK_SIZE - (p * page_size + r), lambda b: rb_ref[b, h])


def _bias_last_block(rel_bias, page_size):
    shape = (N_HEADS, BLOCK_SIZE // page_size, page_size)
    return pl.pallas_call(
        _bias_last_block_kernel,
        grid=(N_HEADS,),
        in_specs=[pl.BlockSpec(memory_space=pltpu.SMEM)],
        out_specs=pl.BlockSpec((1,) + shape[1:], lambda h: (h, 0, 0)),
        out_shape=jax.ShapeDtypeStruct(shape, F32),
        name="bias_last_block",
    )(rel_bias)


def _attn_prompt_kernel(rb_ref, q_ref, k_ref, vt_ref, ksum_ref, bt_ref, o_ref, kb_sc, vt_sc, sel_sc):
    hq = pl.program_id(1)
    qb = pl.program_id(2)
    n_blk = k_ref.shape[1] // BLOCK_SIZE
    n_grp = q_ref.shape[2] // V7X_LANES
    n_heads = n_grp * HEADS_PER_GROUP
    bs = BLOCK_SIZE

    @pl.when(qb == 0)
    def _():
        kb_sc[0:bs, :] = jnp.zeros((bs, kb_sc.shape[1]), BF16)
        kb_sc[bs:, :] = k_ref[0]
        sel_sc[:, n_blk:, :] = jnp.zeros((n_heads, sel_sc.shape[1] - n_blk, bs), F32)
        sub = lax.broadcasted_iota(jnp.int32, (V7X_SUBLANES * 2, bs), 0)
        ones_rows = jnp.where(sub == 0, 1.0, 0.0).astype(BF16)
        for g in range(n_grp):
            vt_sc[g, 0] = jnp.zeros(vt_sc.shape[2:], BF16)
            for j in range(n_blk):
                vt_sc[g, j + 1, 0:V7X_LANES, :] = vt_ref[0, g * V7X_LANES:(g + 1) * V7X_LANES,
                                                         j * bs:(j + 1) * bs].astype(BF16)
                vt_sc[g, j + 1, V7X_LANES:, :] = ones_rows

    q = q_ref[0] * (SCALE * LOG2_E)
    lane = lax.broadcasted_iota(jnp.int32, (bs, V7X_LANES), 1)
    blk_row = lax.broadcasted_iota(jnp.int32, (n_blk, bs), 0)
    qms = []
    for c in range(n_heads):
        g, i = divmod(c, HEADS_PER_GROUP)
        lanes = slice(g * V7X_LANES, (g + 1) * V7X_LANES)
        qm = jnp.where(lane // HEAD_DIM == i, q[:, lanes], 0.0).astype(BF16)
        qms.append(qm)
        st = _dot_nt((ksum_ref[0, :, lanes] / BLOCK_SIZE).astype(BF16), qm)
        rank = jnp.zeros(st.shape, jnp.int32)
        for j in range(n_blk):
            sj = st[j:j + 1, :]
            beats = (sj > st) | ((sj == st) & (j < blk_row))
            rank = rank + jnp.where(beats & (j < qb), 1, 0)
        sel_sc[c, 0:n_blk, :] = jnp.where((rank < TOP_K) & (blk_row < qb), 1.0, 0.0)

    def col_max(x):
        return jnp.max(x, axis=0, keepdims=True)

    def window(c, first_padded_block):
        g = c // HEADS_PER_GROUP
        rows = pl.ds(pl.multiple_of(first_padded_block * bs, bs), 2 * bs)
        s = _dot_nt(kb_sc[rows, g * V7X_LANES:(g + 1) * V7X_LANES], qms[c])
        return s[:bs], s[bs:]

    prev_row = jnp.where(qb == 0, n_blk, qb - 1)
    scores = []
    for c in range(n_heads):
        s_prev, s_own = window(c, qb)
        s_prev = s_prev + bt_ref[c, 1] + jnp.where(sel_sc[c, pl.ds(prev_row, 1), :] > 0.5, 0.0, NEG_INF)
        scores.append((s_prev, s_own + bt_ref[c, 0]))
    states = []
    for c in range(n_heads):
        g = c // HEADS_PER_GROUP
        s_prev, s_own = scores[c]
        m = jnp.maximum(col_max(s_prev), col_max(s_own))
        p_prev, p_own = jnp.exp2(s_prev - m), jnp.exp2(s_own - m)
        acc = _dot(vt_sc[g, qb], p_prev.astype(BF16)) + _dot(vt_sc[g, qb + 1], p_own.astype(BF16))
        states.append((m, acc))

    n_far = jnp.maximum(qb - 1, 0)

    def far_body(t, states):
        j0 = 2 * t
        scores = []
        for c in range(n_heads):
            far_bias = rb_ref[N_BUCKETS - 1, hq * n_heads + c] * LOG2_E
            s0, s1 = window(c, j0 + 1)
            s0 = s0 + jnp.where(sel_sc[c, pl.ds(j0, 1), :] > 0.5, far_bias, NEG_INF)
            s1 = s1 + jnp.where((sel_sc[c, pl.ds(j0 + 1, 1), :] > 0.5) & (j0 + 1 < n_far), far_bias, NEG_INF)
            scores.append((s0, s1))
        new = []
        for c in range(n_heads):
            g = c // HEADS_PER_GROUP
            s0, s1 = scores[c]
            m, acc = states[c]
            m_new = jnp.maximum(m, jnp.maximum(col_max(s0), col_max(s1)))
            alpha = jnp.exp2(m - m_new)
            p0, p1 = jnp.exp2(s0 - m_new), jnp.exp2(s1 - m_new)
            acc_new = (alpha * acc + _dot(vt_sc[g, j0 + 1], p0.astype(BF16))
                       + _dot(vt_sc[g, j0 + 2], p1.astype(BF16)))
            new.append((m_new, acc_new))
        return tuple(new)

    states = lax.fori_loop(0, (n_far + 1) // 2, far_body, tuple(states))
    dim_row = lax.broadcasted_iota(jnp.int32, (V7X_LANES, bs), 0)
    for g in range(n_grp):
        out_t = jnp.zeros((V7X_LANES, bs), F32)
        for i in range(HEADS_PER_GROUP):
            m, acc = states[g * HEADS_PER_GROUP + i]
            out_t = jnp.where(dim_row // HEAD_DIM == i, acc[:V7X_LANES] * (1.0 / acc[V7X_LANES:V7X_LANES + 1]), out_t)
        o_ref[0, :, g * V7X_LANES:(g + 1) * V7X_LANES] = out_t.T


def _attn_prompt(rel_bias, q, k_bf16, v_t, k_sums, bias_tiles):
    batch, seq, d = q.shape
    n_blk = seq // BLOCK_SIZE
    width = ATTN_HEADS * HEAD_DIM
    n_grp = width // V7X_LANES
    q_spec = pl.BlockSpec((1, BLOCK_SIZE, width), lambda b, g, i: (b, i, g))
    return pl.pallas_call(
        _attn_prompt_kernel,
        grid=(batch, d // width, n_blk),
        in_specs=[pl.BlockSpec(memory_space=pltpu.SMEM), q_spec,
                  pl.BlockSpec((1, seq, width), lambda b, g, i: (b, 0, g)),
                  pl.BlockSpec((1, width, seq), lambda b, g, i: (b, g, 0)),
                  pl.BlockSpec((1, n_blk, width), lambda b, g, i: (b, 0, g)),
                  pl.BlockSpec((ATTN_HEADS, 2, BLOCK_SIZE, BLOCK_SIZE), lambda b, g, i: (g, 0, 0, 0))],
        out_specs=q_spec,
        out_shape=jax.ShapeDtypeStruct((batch, seq, d), F32),
        scratch_shapes=[pltpu.VMEM((seq + BLOCK_SIZE, width), BF16),
                        pltpu.VMEM((n_grp, n_blk + 1, V7X_LANES + 2 * V7X_SUBLANES, BLOCK_SIZE), BF16),
                        pltpu.VMEM((ATTN_HEADS, 2 * V7X_SUBLANES, BLOCK_SIZE), F32)],
        compiler_params=pltpu.CompilerParams(dimension_semantics=("arbitrary", "arbitrary", "arbitrary"),
                                             vmem_limit_bytes=V7X_VMEM_LIMIT_BYTES),
        name="attn_prompt",
    )(rel_bias, q, k_bf16, v_t, k_sums, bias_tiles)


def _scan_keys_kernel(pt_ref, qt_ref, ck_ref, lg_ref, sel_ref, pages, qb_sc, pg_sc, sem, *, layer, n_groups):
    b = pl.program_id(0)
    g = pl.program_id(1)
    step = b * n_groups + g
    n_steps = pl.num_programs(0) * n_groups
    page_size = ck_ref.shape[-1]
    pages_per_block = BLOCK_SIZE // page_size
    n_blocks = n_groups * SCAN_PAGES // pages_per_block

    def copies(step, slot):
        return [pltpu.make_async_copy(ck_ref.at[layer, pt_ref[step * SCAN_PAGES + i]], pages.at[slot, i],
                                      sem.at[slot]) for i in range(SCAN_PAGES)]

    slot = step % 2

    @pl.when(step == 0)
    def _():
        for c in copies(0, 0):
            c.start()

    @pl.when(step + 1 < n_steps)
    def _():
        for c in copies(step + 1, 1 - slot):
            c.start()

    @pl.when(g == 0)
    def _():
        qt = qt_ref[0]
        for h in range(N_HEADS):
            qb_sc[h] = jnp.broadcast_to(qt[:, h:h + 1], (HEAD_DIM, page_size))

    for c in copies(step, slot):
        c.wait()

    def page_body(i, carry):
        for h in range(N_HEADS):
            lg_ref[0, h, pl.ds(i, 1), :] = jnp.sum(pages[slot, i, h] * qb_sc[h], axis=0, keepdims=True)
        return carry

    lax.fori_loop(0, SCAN_PAGES, page_body, 0)

    pg_sc[pl.ds(pl.multiple_of(g * SCAN_PAGES, SCAN_PAGES), SCAN_PAGES), :] = _page_sums(lg_ref, SCAN_PAGES)

    @pl.when(g == n_groups - 1)
    def _():
        _topk_blocks(pg_sc, sel_ref, n_blocks, pages_per_block)


def _scan_keys(page_table_flat, q_t, cache_k_t, layer, n_pages):
    n_seq = q_t.shape[0]
    page_size = cache_k_t.shape[-1]
    n_groups = n_pages // SCAN_PAGES
    logits, sel = pl.pallas_call(
        functools.partial(_scan_keys_kernel, layer=layer, n_groups=n_groups),
        grid_spec=pltpu.PrefetchScalarGridSpec(
            num_scalar_prefetch=1,
            grid=(n_seq, n_groups),
            in_specs=[pl.BlockSpec((1, HEAD_DIM, N_HEADS), lambda b, g, pt: (b, 0, 0)),
                      pl.BlockSpec(memory_space=pl.ANY)],
            out_specs=[pl.BlockSpec((1, N_HEADS, SCAN_PAGES, page_size), lambda b, g, pt: (b, 0, g, 0)),
                       pl.BlockSpec((1, V7X_SUBLANES, V7X_LANES), lambda b, g, pt: (b, 0, 0))],
            scratch_shapes=[pltpu.VMEM((2, SCAN_PAGES, N_HEADS, HEAD_DIM, page_size), F32),
                            pltpu.VMEM((N_HEADS, HEAD_DIM, page_size), F32),
                            pltpu.VMEM((n_pages, V7X_LANES), F32),
                            pltpu.SemaphoreType.DMA((2,))]),
        out_shape=[jax.ShapeDtypeStruct((n_seq, N_HEADS, n_pages, page_size), F32),
                   jax.ShapeDtypeStruct((n_seq, V7X_SUBLANES, V7X_LANES), jnp.int32)],
        compiler_params=pltpu.CompilerParams(dimension_semantics=("arbitrary", "arbitrary"),
                                             vmem_limit_bytes=V7X_VMEM_LIMIT_BYTES),
        name="scan_keys",
    )(page_table_flat, q_t, cache_k_t)
    return logits, _sel_to_blocks(sel)


def _attn_sample_kernel(pt_ref, sel_ref, rb_ref, lg_ref, q_ref, kn_ref, vnt_ref, bl_ref, cv_ref, ot_ref,
                        vbuf, sem, *, layer, n_pages):
    b = pl.program_id(0)
    n_seq = pl.num_programs(0)
    page_size = cv_ref.shape[-1]
    pages_per_block = BLOCK_SIZE // page_size
    last_block = n_pages // pages_per_block - 1

    def selected(seq, h, s):
        return sel_ref[(seq * N_HEADS + h) * TOP_K + s]

    def copies(seq, slot):
        out = []
        for h in range(N_HEADS):
            for s in range(TOP_K):
                blk = selected(seq, h, s)
                for p in range(pages_per_block):
                    phys = pt_ref[seq * n_pages + blk * pages_per_block + p]
                    out.append(pltpu.make_async_copy(cv_ref.at[layer, phys, h],
                                                     vbuf.at[slot, h, s * pages_per_block + p], sem.at[slot]))
        return out

    slot = b % 2

    @pl.when(b == 0)
    def _():
        for c in copies(0, 0):
            c.start()

    @pl.when(b + 1 < n_seq)
    def _():
        for c in copies(b + 1, 1 - slot):
            c.start()

    for c in copies(b, slot):
        c.wait()

    def all_max(x):
        return jnp.max(jnp.max(x, axis=1, keepdims=True), axis=0, keepdims=True)

    def all_sum(x):
        return jnp.sum(jnp.sum(x, axis=1, keepdims=True), axis=0, keepdims=True)

    new_dots = jnp.sum(q_ref[0] * kn_ref[0], axis=1, keepdims=True) * SCALE
    for h in range(N_HEADS):
        logit_new = new_dots[h:h + 1, :] + rb_ref[0, h]
        pieces = []
        for s in range(TOP_K):
            blk = selected(b, h, s)
            raw = lg_ref[0, h, pl.ds(pl.multiple_of(blk * pages_per_block, pages_per_block), pages_per_block), :]
            bias = jnp.where(blk == last_block, bl_ref[h], rb_ref[N_BUCKETS - 1, h])
            pieces.append(raw * SCALE + bias)
        m = logit_new
        for x in pieces:
            m = jnp.maximum(m, all_max(x))
        p_new = jnp.exp(logit_new - m)
        denom = p_new
        acc = None
        for s, x in enumerate(pieces):
            p = jnp.exp(x - m)
            denom = denom + all_sum(p)
            for pg in range(pages_per_block):
                term = p[pg:pg + 1, :] * vbuf[slot, h, s * pages_per_block + pg]
                acc = term if acc is None else acc + term
        num = jnp.sum(acc, axis=1, keepdims=True) + p_new * vnt_ref[0, :, h:h + 1]
        ot_ref[0, :, h:h + 1] = num / denom


def _attn_sample(page_table_flat, sel_flat, rel_bias, logits, q, k_new, v_new_t, bias_last, cache_v_t, layer,
                 n_pages):
    n_seq = q.shape[0]
    page_size = cache_v_t.shape[-1]
    pages_per_block = BLOCK_SIZE // page_size
    seq_spec = pl.BlockSpec((1, N_HEADS, HEAD_DIM), lambda b, pt, sel: (b, 0, 0))
    seq_t_spec = pl.BlockSpec((1, HEAD_DIM, N_HEADS), lambda b, pt, sel: (b, 0, 0))
    return pl.pallas_call(
        functools.partial(_attn_sample_kernel, layer=layer, n_pages=n_pages),
        grid_spec=pltpu.PrefetchScalarGridSpec(
            num_scalar_prefetch=2,
            grid=(n_seq,),
            in_specs=[pl.BlockSpec(memory_space=pltpu.SMEM),
                      pl.BlockSpec((1, N_HEADS, n_pages, page_size), lambda b, pt, sel: (b, 0, 0, 0)),
                      seq_spec, seq_spec, seq_t_spec,
                      pl.BlockSpec(bias_last.shape, lambda b, pt, sel: (0, 0, 0)),
                      pl.BlockSpec(memory_space=pl.ANY)],
            out_specs=seq_t_spec,
            scratch_shapes=[pltpu.VMEM((2, N_HEADS, TOP_K * pages_per_block, HEAD_DIM, page_size), F32),
                            pltpu.SemaphoreType.DMA((2,))]),
        out_shape=jax.ShapeDtypeStruct((n_seq, HEAD_DIM, N_HEADS), F32),
        compiler_params=pltpu.CompilerParams(dimension_semantics=("arbitrary",),
                                             vmem_limit_bytes=V7X_VMEM_LIMIT_BYTES),
        name="attn_sample",
    )(page_table_flat, sel_flat, rel_bias, logits, q, k_new, v_new_t, bias_last, cache_v_t)


def _mix_out_kernel(*refs, final, scan_layer):
    if scan_layer is None:
        x_ref, oa_ref, ga_ref, zc_ref, wba_ref, wout_ref, g2_ref, wg_ref, wu_ref, wd_ref, gf_ref, y_ref = refs
        side_work = finish = None
    else:
        (pt_ref, x_ref, oa_ref, ga_ref, zc_ref, wba_ref, wout_ref, g2_ref, wg_ref, wu_ref, wd_ref, gf_ref,
         qt_ref, ck_ref, y_ref, lg_ref, sel_ref) = refs[:17]
        side_work, finish = _hosted_scan(pt_ref, qt_ref, ck_ref, lg_ref, sel_ref, *refs[17:], layer=scan_layer)
    y_attn = _dot(oa_ref[...].astype(BF16), wba_ref[...])
    merged = ga_ref[...] * y_attn + zc_ref[...]
    x = x_ref[...] + _dot(merged.astype(BF16), wout_ref[...])
    h = _rms(x, g2_ref[...]).astype(BF16)
    x = x + 0.5 * _swiglu(h, wg_ref, wu_ref, wd_ref, side_work)
    y_ref[...] = _rms(x, gf_ref[...]) if final else x
    if finish is not None:
        finish()


def _mix_out(x, oa, ga, zc, wba, wout, g2, wg, wu, wd, gf, rows, final, scan=None):
    n, d = x.shape
    row_spec = pl.BlockSpec((rows, d), lambda i, *_: (i, 0))
    in_specs = [row_spec] * 4 + [_const_spec(wba.shape), _const_spec(wout.shape), _const_spec((1, d)),
                                 _const_spec(wg.shape), _const_spec(wu.shape), _const_spec(wd.shape),
                                 _const_spec((1, d))]
    out_specs, out_shape, scratch, prefetch, operands = [row_spec], [jax.ShapeDtypeStruct((n, d), F32)], [], [], []
    if scan is not None:
        pt, q_t, cache_k_t, layer = scan
        assert q_t.shape[0] == n // rows
        extra = _hosted_scan_specs(q_t.shape[0], pt.shape[0] // q_t.shape[0], cache_k_t.shape[-1])
        in_specs, out_specs, out_shape, scratch = (in_specs + extra[0], out_specs + extra[1], out_shape + extra[2],
                                                   extra[3])
        prefetch, operands = [pt], [q_t, cache_k_t]
    outs = pl.pallas_call(
        functools.partial(_mix_out_kernel, final=final, scan_layer=None if scan is None else scan[3]),
        grid_spec=pltpu.PrefetchScalarGridSpec(num_scalar_prefetch=len(prefetch), grid=(n // rows,),
                                               in_specs=in_specs, out_specs=out_specs, scratch_shapes=scratch),
        out_shape=out_shape,
        compiler_params=pltpu.CompilerParams(dimension_semantics=("arbitrary",),
                                             vmem_limit_bytes=V7X_VMEM_LIMIT_BYTES),
        name="mix_out",
    )(*prefetch, x, oa, ga, zc, wba, wout, g2, wg, wu, wd, gf, *operands)
    return outs[0] if scan is None else (outs[0], outs[1], _sel_to_blocks(outs[2]))


def kernel(x_prompt, x_sample, cache_k, cache_v, state_conv, page_table, rel_bias, norm_ffn1, w_ffn1_gate,
           w_ffn1_up, w_ffn1_down, norm_mix, w_in, w_conv, w_branch_attn, w_branch_conv, w_out, norm_ffn2,
           w_ffn2_gate, w_ffn2_up, w_ffn2_down, norm_final):
    batch, seq, d = x_prompt.shape
    n_seq, t_new, _ = x_sample.shape
    depth = w_in.shape[0]
    n_pages = page_table.shape[1]
    page_size = cache_k.shape[2]
    assert t_new == 1 and d == N_HEADS * HEAD_DIM and w_in.shape[2] == 8 * d
    assert seq % PROMPT_ROWS == 0 and PROMPT_ROWS % BLOCK_SIZE == 0 and N_HEADS % ATTN_HEADS == 0
    assert (n_pages * page_size) % BLOCK_SIZE == 0 and n_pages % SCAN_PAGES == 0 and BLOCK_SIZE % page_size == 0
    assert MAX_DISTANCE <= BLOCK_SIZE and n_pages % SCAN_RING == 0 and seq % MIX_OUT_ROWS == 0
    assert (batch * seq) // PROMPT_ROWS + (batch * seq) // MIX_OUT_ROWS < n_seq

    xp = x_prompt.reshape(batch * seq, d)
    xs = x_sample.reshape(n_seq, d)
    pt_flat = page_table.reshape(-1)
    cache_k_t = jnp.transpose(cache_k, (0, 1, 3, 4, 2))
    cache_v_t = jnp.transpose(cache_v, (0, 1, 3, 4, 2))
    bias_tiles = _bias_tiles(rel_bias)
    bias_last = _bias_last_block(rel_bias, page_size)
    outs = {name: [] for name in ("kp", "vp", "cp", "ks", "vs", "cs")}
    for layer in range(depth):
        g1, gm, g2 = (g[layer].reshape(1, d) for g in (norm_ffn1, norm_mix, norm_ffn2))
        gf = norm_final.reshape(1, d)
        wg1, wu1, wd1, win, wba, wbc, wout, wg2, wu2, wd2 = (
            w[layer].astype(BF16) for w in (w_ffn1_gate, w_ffn1_up, w_ffn1_down, w_in, w_branch_attn,
                                            w_branch_conv, w_out, w_ffn2_gate, w_ffn2_up, w_ffn2_down))
        wconv = w_conv[layer]
        final = layer == depth - 1

        xs = _ffn(xs, g1, wg1, wu1, wd1, n_seq)
        prev2, prev1 = state_conv[layer, :, 0, :], state_conv[layer, :, 1, :]
        qs, ks, vs, gas, zcs, cu = _mix_in_sample(xs, gm, win, wconv, wbc, prev2, prev1)
        q3, k3, v3 = (a.reshape(n_seq, N_HEADS, HEAD_DIM) for a in (qs, ks, vs))
        q3_t, v3_t = jnp.transpose(q3, (0, 2, 1)), jnp.transpose(v3, (0, 2, 1))
        bounds = [0, (batch * seq) // PROMPT_ROWS, (batch * seq) // PROMPT_ROWS + (batch * seq) // MIX_OUT_ROWS, n_seq]
        pages = lambda r: pt_flat[bounds[r] * n_pages:bounds[r + 1] * n_pages]
        hosted = lambda r: (pages(r), q3_t[bounds[r]:bounds[r + 1]], cache_k_t, layer)

        xp, *scan0 = _ffn(xp, g1, wg1, wu1, wd1, PROMPT_ROWS, hosted(0))
        q, k_t, v_t, k_bf16, k_sums, ga, zc, conv_state = _mix_in_prompt(xp, gm, win, wconv, wbc, batch, seq,
                                                                        PROMPT_ROWS)
        oa = _attn_prompt(rel_bias, q.reshape(batch, seq, d), k_bf16.reshape(batch, seq, d), v_t,
                          k_sums.reshape(batch, seq // BLOCK_SIZE, d), bias_tiles).reshape(batch * seq, d)
        xp, *scan1 = _mix_out(xp, oa, ga, zc, wba, wout, g2, wg2, wu2, wd2, gf, MIX_OUT_ROWS, final, hosted(1))
        for name, a_t in (("kp", k_t), ("vp", v_t)):
            outs[name].append(jnp.transpose(a_t.reshape(batch, N_HEADS, HEAD_DIM, seq), (0, 3, 1, 2)))
        outs["cp"].append(conv_state)

        scan2 = _scan_keys(pages(2), q3_t[bounds[2]:], cache_k_t, layer, n_pages)
        oa_parts = []
        for r, (logits, sel) in enumerate((scan0, scan1, scan2)):
            lo, hi = bounds[r], bounds[r + 1]
            oa_parts.append(_attn_sample(pages(r), sel.reshape(-1), rel_bias, logits, q3[lo:hi], k3[lo:hi],
                                         v3_t[lo:hi], bias_last, cache_v_t, layer, n_pages))
        oa = jnp.transpose(jnp.concatenate(oa_parts, axis=0), (0, 2, 1)).reshape(n_seq, d)
        xs = _mix_out(xs, oa, gas, zcs, wba, wout, g2, wg2, wu2, wd2, gf, n_seq, final)
        outs["ks"].append(k3.reshape(n_seq, 1, N_HEADS, HEAD_DIM))
        outs["vs"].append(v3.reshape(n_seq, 1, N_HEADS, HEAD_DIM))
        outs["cs"].append(jnp.stack([prev1, cu], axis=1))

    return (xp.reshape(batch, seq, d), xs.reshape(n_seq, 1, d), jnp.stack(outs["kp"]), jnp.stack(outs["vp"]),
            jnp.stack(outs["cp"]), jnp.stack(outs["ks"]), jnp.stack(outs["vs"]), jnp.stack(outs["cs"]))
```
